```python
import math, functools
import jax, jax.numpy as jnp
from jax import lax
import numpy as np

D_MODEL = 1024
BATCH = 8
SEQ = 4096
DEPTH = 1
DEC_BATCH = 32
DEC_SEQ = 8
PAST_LEN = 16384
PAGE_SIZE = 128

HEAD_DIM = 64
MIX_WIDTH = D_MODEL
RWKV_WIDTH = MIX_WIDTH // 2
ATTN_WIDTH = MIX_WIDTH - RWKV_WIDTH
RWKV_HEADS = RWKV_WIDTH // HEAD_DIM
ATTN_HEADS = ATTN_WIDTH // HEAD_DIM
DECAY_LORA = 32
AAA_LORA = 32
GATE_LORA = 96
RWKV_PROJ = 3 * RWKV_WIDTH + DECAY_LORA + AAA_LORA + GATE_LORA
IN_PROJ = RWKV_PROJ + 3 * ATTN_WIDTH
DILATED = ((128, 1), (512, 4), (2048, 16))
WIN_MAX = 2048
ROPE_THETA = 10000.0
N_GROUPS = 4
EXPERTS_PER_GROUP = 8
N_EXPERTS = N_GROUPS * EXPERTS_PER_GROUP
TOP_K_INNER = 2
EXPERT_FF = 256
PLE_DIM = 256
RMS_EPS = 1e-6
LNX_EPS = 1e-5 * HEAD_DIM
NEG_INF = -1e30

kernel_name = 'hymba_rwkv7_dilated_hmoe_step'


def _rmsnorm(x, g):
    xf = x.astype(jnp.float32)
    y = xf * lax.rsqrt(jnp.mean(xf * xf, axis=-1, keepdims=True) + RMS_EPS)
    return (y * g.astype(jnp.float32)).astype(x.dtype)


def _rope(x, pos):
    half = HEAD_DIM // 2
    inv = jnp.power(ROPE_THETA, -jnp.arange(half, dtype=jnp.float32) * 2.0 / HEAD_DIM)
    ang = pos[:, None] * inv[None, :]
    cos = jnp.cos(ang)[None, :, None, :]
    sin = jnp.sin(ang)[None, :, None, :]
    xf = x.astype(jnp.float32)
    x1, x2 = xf[..., :half], xf[..., half:]
    return jnp.concatenate([x1 * cos - x2 * sin, x2 * cos + x1 * sin], axis=-1).astype(x.dtype)


def _branch_prompt(q, k, v, window, dil):
    B, S, H, E = q.shape
    L = window // dil
    unit = L * dil
    s_pad = -(-S // unit) * unit
    n = s_pad // unit

    def blocks(t):
        t = jnp.pad(t, ((0, 0), (0, s_pad - S), (0, 0), (0, 0)))
        return t.reshape(B, n, L, dil, H, E)

    def with_prev(t):
        prev = jnp.pad(t, ((0, 0), (1, 0), (0, 0), (0, 0), (0, 0), (0, 0)))[:, :-1]
        return jnp.concatenate([prev, t], axis=2)

    qb = blocks(q).astype(jnp.float32)
    kb = with_prev(blocks(k)).astype(jnp.float32)
    vb = with_prev(blocks(v)).astype(jnp.float32)
    s = jnp.einsum('bnidhe,bnjdhe->bndhij', qb, kb) / math.sqrt(E)
    i = jnp.arange(L)[:, None]
    j = jnp.arange(2 * L)[None, :]
    dist = i + L - j
    band = (dist >= 0) & (dist <= L)
    first = (jnp.arange(n)[:, None] * L + jnp.arange(2 * L)[None, :] - L) >= 0
    mask = band[None, :, :] & first[:, None, :]
    s = jnp.where(mask[None, :, None, None, :, :], s, NEG_INF)
    m = jnp.max(s, axis=-1, keepdims=True)
    e = jnp.exp(s - m)
    l = jnp.sum(e, axis=-1)
    o = jnp.einsum('bndhij,bnjdhe->bnidhe', e, vb)
    l_t = jnp.transpose(l, (0, 1, 4, 2, 3))
    m_t = jnp.transpose(m[..., 0], (0, 1, 4, 2, 3))
    o = (o / l_t[..., None]).reshape(B, s_pad, H, E)[:, :S]
    lse = (m_t + jnp.log(l_t)).reshape(B, s_pad, H)[:, :S]
    return o, lse


def _branch_decode(q, kc, vc, window, dil, wc):
    B, T, H, E = q.shape
    nk = window // dil + 1
    idx = wc + jnp.arange(T)[:, None] - jnp.arange(nk)[None, :] * dil
    valid = idx >= 0
    idxc = jnp.maximum(idx, 0)
    kg = kc[:, idxc].astype(jnp.float32)
    vg = vc[:, idxc].astype(jnp.float32)
    s = jnp.einsum('bthe,btkhe->bthk', q.astype(jnp.float32), kg) / math.sqrt(E)
    s = jnp.where(valid[None, :, None, :], s, NEG_INF)
    m = jnp.max(s, axis=-1, keepdims=True)
    e = jnp.exp(s - m)
    l = jnp.sum(e, axis=-1)
    o = jnp.einsum('bthk,btkhe->bthe', e, vg) / l[..., None]
    return o, m[..., 0] + jnp.log(l)


def _merge(outs, lses):
    w = jax.nn.softmax(jnp.stack(lses, axis=0), axis=0)
    return jnp.sum(w[..., None] * jnp.stack(outs, axis=0), axis=0)


def _dilated_prompt(q, k, v):
    outs, lses = [], []
    for window, dil in DILATED:
        o, l = _branch_prompt(q, k, v, window, dil)
        outs.append(o)
        lses.append(l)
    return _merge(outs, lses).astype(q.dtype)


def _dilated_decode(q, k, v, k_buf, v_buf):
    kc = jnp.concatenate([k_buf.astype(k.dtype), k], axis=1)
    vc = jnp.concatenate([v_buf.astype(v.dtype), v], axis=1)
    wc = k_buf.shape[1]
    outs, lses = [], []
    for window, dil in DILATED:
        o, l = _branch_decode(q, kc, vc, window, dil, wc)
        outs.append(o)
        lses.append(l)
    return _merge(outs, lses).astype(q.dtype)


def _wkv_step(S, inp):
    r_t, w_t, k_t, v_t, a_t, b_t = inp
    sa = jnp.einsum('bhvk,bhk->bhv', S, a_t)
    S = S * w_t[:, :, None, :] + sa[..., None] * b_t[:, :, None, :] + v_t[..., None] * k_t[:, :, None, :]
    y = jnp.einsum('bhvk,bhk->bhv', S, r_t)
    return S, y


def _rwkv7(P, shift_state, wkv_state, mu, w0, w2, a0, a2, g2, k_k, k_a, r_k, lnx_w, lnx_b):
    B, T, _ = P.shape
    H, E, C = RWKV_HEADS, HEAD_DIM, RWKV_WIDTH
    Pf = P.astype(jnp.float32)
    prev = jnp.concatenate([shift_state.astype(jnp.float32)[:, None], Pf[:, :-1]], axis=1)
    Pm = Pf + (prev - Pf) * mu
    r = Pm[..., :C]
    k = Pm[..., C:2 * C]
    v = Pm[..., 2 * C:3 * C]
    o = 3 * C
    xw = Pm[..., o:o + DECAY_LORA]
    o += DECAY_LORA
    xa = Pm[..., o:o + AAA_LORA]
    o += AAA_LORA
    xg = Pm[..., o:o + GATE_LORA]
    w = -jax.nn.softplus(-(w0 + jnp.tanh(xw) @ w2)) - 0.5
    decay = jnp.exp(-jnp.exp(w))
    a = jax.nn.sigmoid(a0 + xa @ a2)
    g = jax.nn.sigmoid(xg) @ g2
    kk = (k * k_k).reshape(B, T, H, E)
    kk = kk / jnp.maximum(jnp.sqrt(jnp.sum(kk * kk, axis=-1, keepdims=True)), 1e-12)
    k = k * (1.0 + (a - 1.0) * k_a)
    r_h = r.reshape(B, T, H, E)
    k_h = k.reshape(B, T, H, E)
    v_h = v.reshape(B, T, H, E)
    d_h = decay.reshape(B, T, H, E)
    a_h = a.reshape(B, T, H, E)
    xs = tuple(jnp.moveaxis(t, 1, 0) for t in (r_h, d_h, k_h, v_h, -kk, kk * a_h))
    S_fin, ys = lax.scan(_wkv_step, wkv_state.astype(jnp.float32), xs)
    y = jnp.moveaxis(ys, 0, 1)
    mean = jnp.mean(y, axis=-1, keepdims=True)
    var = jnp.mean(jnp.square(y - mean), axis=-1, keepdims=True)
    yn = (y - mean) * lax.rsqrt(var + LNX_EPS) * lnx_w.reshape(H, E) + lnx_b.reshape(H, E)
    bonus = jnp.sum(r_h * k_h * r_k, axis=-1, keepdims=True) * v_h
    out = ((yn + bonus).reshape(B, T, C) * g).astype(P.dtype)
    return out, P[:, -1], S_fin


def _hier_moe(u, rgw, rgb, rew, reb, ewg, ewu, ewd):
    B, T, D = u.shape
    x = u.reshape(B * T, D)
    lg = (x @ rgw).astype(jnp.float32) + rgb
    pg = jax.nn.softmax(lg, axis=-1)
    gsel = jnp.argmax(lg, axis=-1)
    pg_sel = jnp.take_along_axis(pg, gsel[:, None], axis=1)[:, 0]
    le = ((x @ rew).astype(jnp.float32) + reb).reshape(-1, N_GROUPS, EXPERTS_PER_GROUP)
    le_sel = jnp.take_along_axis(le, gsel[:, None, None], axis=1)[:, 0]
    pe = jax.nn.softmax(le_sel, axis=-1)
    top_p, top_i = lax.top_k(pe, TOP_K_INNER)
    wts = pg_sel[:, None] * top_p / jnp.sum(top_p, axis=-1, keepdims=True)
    eid = gsel[:, None] * EXPERTS_PER_GROUP + top_i
    gate = jnp.sum(jax.nn.one_hot(eid, N_EXPERTS, dtype=jnp.float32) * wts[..., None], axis=1)
    gate = gate.astype(x.dtype)
    y = jnp.zeros_like(x)
    for gi in range(N_GROUPS):
        sl = slice(gi * EXPERTS_PER_GROUP, (gi + 1) * EXPERTS_PER_GROUP)
        hg = jax.nn.silu(jnp.einsum('nd,edf->nef', x, ewg[sl])) * jnp.einsum('nd,edf->nef', x, ewu[sl])
        y = y + jnp.einsum('nef,efd->nd', hg * gate[:, sl, None], ewd[sl])
    return y.reshape(B, T, D)


def _layer(h, p, attend, pos, shift_state, wkv_state,
           norm_mix, w_in, mu_shift, decay_w0, decay_w2, iclr_a0, iclr_a2, gate_g2,
           k_k, k_a, r_k, lnx_w, lnx_b, attn_out_norm, w_out, norm_ffn,
           router_group_w, router_group_b, router_expert_w, router_expert_b,
           expert_w_gate, expert_w_up, expert_w_down, norm_ple, ple_gate_w, ple_proj):
    B, T, _ = h.shape
    u = _rmsnorm(h, norm_mix)
    proj = u @ w_in
    P = proj[..., :RWKV_PROJ]
    o = RWKV_PROJ
    q = proj[..., o:o + ATTN_WIDTH].reshape(B, T, ATTN_HEADS, HEAD_DIM)
    k = proj[..., o + ATTN_WIDTH:o + 2 * ATTN_WIDTH].reshape(B, T, ATTN_HEADS, HEAD_DIM)
    v = proj[..., o + 2 * ATTN_WIDTH:o + 3 * ATTN_WIDTH].reshape(B, T, ATTN_HEADS, HEAD_DIM)
    q = _rope(q, pos)
    k = _rope(k, pos)
    rw_out, new_shift, new_wkv = _rwkv7(P, shift_state, wkv_state, mu_shift, decay_w0, decay_w2,
                                        iclr_a0, iclr_a2, gate_g2, k_k, k_a, r_k, lnx_w, lnx_b)
    att = attend(q, k, v).reshape(B, T, ATTN_WIDTH)
    att = _rmsnorm(att, attn_out_norm)
    h = h + jnp.concatenate([rw_out, att], axis=-1) @ w_out
    h = h + _hier_moe(_rmsnorm(h, norm_ffn), router_group_w, router_group_b, router_expert_w,
                      router_expert_b, expert_w_gate, expert_w_up, expert_w_down)
    h = h + jax.nn.sigmoid(_rmsnorm(h, norm_ple) @ ple_gate_w) * (p @ ple_proj)
    return h, k, v, new_shift, new_wkv


def setup_inputs(seed: int = 0) -> dict:
    key = jax.random.key(seed)
    ks = iter(jax.random.split(key, 48))
    f32 = jnp.float32

    def nrm(shape, scale):
        return jax.random.normal(next(ks), shape, f32) * scale

    def gain(shape):
        return 1.0 + nrm(shape, 0.05)

    wc = min(WIN_MAX, PAST_LEN)
    d = {}
    d['x_prompt'] = nrm((BATCH, SEQ, D_MODEL), 1.0)
    d['x_sample'] = nrm((DEC_BATCH, DEC_SEQ, D_MODEL), 1.0)
    d['cache_k_win'] = nrm((DEPTH, DEC_BATCH, wc, ATTN_HEADS, HEAD_DIM), 1.0)
    d['cache_v_win'] = nrm((DEPTH, DEC_BATCH, wc, ATTN_HEADS, HEAD_DIM), 1.0)
    d['state_wkv'] = nrm((DEPTH, DEC_BATCH, RWKV_HEADS, HEAD_DIM, HEAD_DIM), 0.3)
    d['state_shift'] = nrm((DEPTH, DEC_BATCH, RWKV_PROJ), 1.0)
    d['p_prompt'] = nrm((DEPTH, BATCH, SEQ, PLE_DIM), 1.0)
    d['p_sample'] = nrm((DEPTH, DEC_BATCH, DEC_SEQ, PLE_DIM), 1.0)
    d['norm_mix'] = gain((DEPTH, D_MODEL))
    d['w_in'] = nrm((DEPTH, D_MODEL, IN_PROJ), D_MODEL ** -0.5)
    d['mu_shift'] = jax.random.uniform(next(ks), (DEPTH, RWKV_PROJ), f32)
    d['decay_w0'] = jax.random.uniform(next(ks), (DEPTH, RWKV_WIDTH), f32, -6.0, 1.0)
    d['decay_w2'] = nrm((DEPTH, DECAY_LORA, RWKV_WIDTH), 0.1)
    d['iclr_a0'] = nrm((DEPTH, RWKV_WIDTH), 0.1)
    d['iclr_a2'] = nrm((DEPTH, AAA_LORA, RWKV_WIDTH), 0.5 * AAA_LORA ** -0.5)
    d['gate_g2'] = nrm((DEPTH, GATE_LORA, RWKV_WIDTH), GATE_LORA ** -0.5)
    d['k_k'] = 0.85 + nrm((DEPTH, RWKV_WIDTH), 0.05)
    d['k_a'] = gain((DEPTH, RWKV_WIDTH))
    d['r_k'] = nrm((DEPTH, RWKV_HEADS, HEAD_DIM), 0.1)
    d['lnx_w'] = gain((DEPTH, RWKV_WIDTH))
    d['lnx_b'] = nrm((DEPTH, RWKV_WIDTH), 0.01)
    d['attn_out_norm'] = gain((DEPTH, ATTN_WIDTH))
    d['w_out'] = nrm((DEPTH, MIX_WIDTH, D_MODEL), MIX_WIDTH ** -0.5)
    d['norm_ffn'] = gain((DEPTH, D_MODEL))
    d['router_group_w'] = nrm((DEPTH, D_MODEL, N_GROUPS), D_MODEL ** -0.5)
    d['router_group_b'] = nrm((DEPTH, N_GROUPS), 0.01)
    d['router_expert_w'] = nrm((DEPTH, D_MODEL, N_EXPERTS), D_MODEL ** -0.5)
    d['router_expert_b'] = nrm((DEPTH, N_EXPERTS), 0.01)
    d['expert_w_gate'] = nrm((DEPTH, N_EXPERTS, D_MODEL, EXPERT_FF), D_MODEL ** -0.5)
    d['expert_w_up'] = nrm((DEPTH, N_EXPERTS, D_MODEL, EXPERT_FF), D_MODEL ** -0.5)
    d['expert_w_down'] = nrm((DEPTH, N_EXPERTS, EXPERT_FF, D_MODEL), EXPERT_FF ** -0.5)
    d['norm_ple'] = gain((DEPTH, D_MODEL))
    d['ple_gate_w'] = nrm((DEPTH, D_MODEL, D_MODEL), D_MODEL ** -0.5)
    d['ple_proj'] = nrm((DEPTH, PLE_DIM, D_MODEL), PLE_DIM ** -0.5)
    d['norm_final'] = gain((D_MODEL,))
    return d


def reference(x_prompt, x_sample, cache_k_win, cache_v_win, state_wkv, state_shift,
              p_prompt, p_sample, norm_mix, w_in, mu_shift, decay_w0, decay_w2, iclr_a0,
              iclr_a2, gate_g2, k_k, k_a, r_k, lnx_w, lnx_b, attn_out_norm, w_out, norm_ffn,
              router_group_w, router_group_b, router_expert_w, router_expert_b,
              expert_w_gate, expert_w_up, expert_w_down, norm_ple, ple_gate_w, ple_proj,
              norm_final):
    B, S, _ = x_prompt.shape
    DB, T, _ = x_sample.shape
    keep = min(WIN_MAX, S)
    pos_prompt = jnp.arange(S, dtype=jnp.float32)
    pos_sample = PAST_LEN + jnp.arange(T, dtype=jnp.float32)
    hp, hs = x_prompt, x_sample
    pk, pv, pw, psh, sk, sv, sw, ssh = [], [], [], [], [], [], [], []
    for i in range(DEPTH):
        lp = (norm_mix[i], w_in[i], mu_shift[i], decay_w0[i], decay_w2[i], iclr_a0[i], iclr_a2[i],
              gate_g2[i], k_k[i], k_a[i], r_k[i], lnx_w[i], lnx_b[i], attn_out_norm[i], w_out[i],
              norm_ffn[i], router_group_w[i], router_group_b[i], router_expert_w[i],
              router_expert_b[i], expert_w_gate[i], expert_w_up[i], expert_w_down[i],
              norm_ple[i], ple_gate_w[i], ple_proj[i])
        zero_shift = jnp.zeros((B, RWKV_PROJ), hp.dtype)
        zero_wkv = jnp.zeros((B, RWKV_HEADS, HEAD_DIM, HEAD_DIM), jnp.float32)
        hp, kp_, vp_, shp_, wkp_ = _layer(hp, p_prompt[i], _dilated_prompt, pos_prompt,
                                          zero_shift, zero_wkv, *lp)
        attend_s = functools.partial(_dilated_decode, k_buf=cache_k_win[i], v_buf=cache_v_win[i])
        hs, ks_, vs_, shs_, wks_ = _layer(hs, p_sample[i], attend_s, pos_sample,
                                          state_shift[i], state_wkv[i], *lp)
        pk.append(kp_[:, S - keep:])
        pv.append(vp_[:, S - keep:])
        pw.append(wkp_.astype(x_prompt.dtype))
        psh.append(shp_)
        sk.append(ks_)
        sv.append(vs_)
        sw.append(wks_.astype(state_wkv.dtype))
        ssh.append(shs_.astype(state_shift.dtype))
    y_prompt = _rmsnorm(hp, norm_final)
    y_sample = _rmsnorm(hs, norm_final)
    prompt_k_win = jnp.stack(pk, axis=0)
    prompt_v_win = jnp.stack(pv, axis=0)
    prompt_wkv = jnp.stack(pw, axis=0)
    prompt_shift = jnp.stack(psh, axis=0)
    sample_k_rows = jnp.stack(sk, axis=0)
    sample_v_rows = jnp.stack(sv, axis=0)
    sample_wkv = jnp.stack(sw, axis=0)
    sample_shift = jnp.stack(ssh, axis=0)
    return (y_prompt, y_sample, prompt_k_win, prompt_v_win, prompt_wkv, prompt_shift,
            sample_k_rows, sample_v_rows, sample_wkv, sample_shift)
```

```python
import functools
import math

import jax
import jax.numpy as jnp
from jax import lax
from jax.experimental import pallas as pl
from jax.experimental.pallas import tpu as pltpu

F32 = jnp.float32
BF16 = jnp.bfloat16

HEAD_DIM = 64
DECAY_LORA = 32
AAA_LORA = 32
GATE_LORA = 96
LORA_PAD = 256
DILATED = ((128, 1), (512, 4), (2048, 16))
WIN_MAX = 2048
ROPE_THETA = 10000.0
N_GROUPS = 4
EXPERTS_PER_GROUP = 8
N_EXPERTS = N_GROUPS * EXPERTS_PER_GROUP
PAST_LEN = 16384
RMS_EPS = 1e-6
LNX_EPS = 1e-5 * HEAD_DIM
NEG_INF = -1e30
LANES = 128
SUBLANES = 8
VMEM_LIMIT = 52 * 1024 * 1024


def _dot(a, b, precise):
    if precise:
        return jnp.dot(a, b, preferred_element_type=F32, precision=lax.Precision.HIGHEST)
    return jnp.dot(a.astype(BF16), b.astype(BF16), preferred_element_type=F32)


def _dot_nt(a, b, precise):
    dn = (((1,), (1,)), ((), ()))
    if precise:
        return lax.dot_general(a, b, dn, preferred_element_type=F32, precision=lax.Precision.HIGHEST)
    return lax.dot_general(a.astype(BF16), b.astype(BF16), dn, preferred_element_type=F32)


def _div_pow2(x, n):
    assert n & (n - 1) == 0
    return lax.shift_right_arithmetic(x, jnp.int32(n.bit_length() - 1))


def _mod_pow2(x, n):
    assert n & (n - 1) == 0
    return jnp.bitwise_and(x, n - 1)


def _sigmoid(x):
    return 1.0 / (1.0 + jnp.exp(-x))


def _rms(x, g):
    return x * lax.rsqrt(jnp.mean(x * x, axis=-1, keepdims=True) + RMS_EPS) * g


def _inproj_kernel(x_ref, nw_ref, w_ref, mu_ref, sprev_ref, cos_ref, sin_ref,
                   w0_ref, w2_ref, a0_ref, a2_ref, g2_ref,
                   r_ref, k_ref, v_ref, dec_ref, a_ref, g_ref, q_ref, ka_ref, va_ref, plast_ref,
                   carry_ref, *, precise, seq_len, tiles_per_seq, cw, cp):
    tm = x_ref.shape[0]
    u = _rms(x_ref[...], nw_ref[...])
    proj = _dot(u, w_ref[...], precise)
    pc = proj[:, :cp]
    row = lax.broadcasted_iota(jnp.int32, (tm, 1), 0)
    prev = pltpu.roll(pc, 1, axis=0)
    if tiles_per_seq is None:
        prev = jnp.where(_mod_pow2(row, seq_len) == 0, sprev_ref[...], prev)
        plast_ref[...] = pc
    else:
        j = pl.program_id(0) % tiles_per_seq
        first = jnp.where(j == 0, sprev_ref[0], carry_ref[...])
        prev = jnp.where(row == 0, first, prev)
        carry_ref[...] = pc[tm - 1:tm, :]
        plast_ref[0] = pc[tm - 1:tm, :]
    pm = pc + (prev - pc) * mu_ref[...]
    r_ref[...] = pm[:, 0:cw]
    k_ref[...] = pm[:, cw:2 * cw]
    v_ref[...] = pm[:, 2 * cw:3 * cw]
    lora = pm[:, 3 * cw:cp]
    zw = w0_ref[...] + _dot(jnp.tanh(lora), w2_ref[...], precise)
    nz = -zw
    softplus = jnp.maximum(nz, 0.0) + jnp.log(1.0 + jnp.exp(-jnp.abs(nz)))
    dec_ref[...] = jnp.exp(-jnp.exp(-softplus - 0.5))
    a_ref[...] = _sigmoid(a0_ref[...] + _dot(lora, a2_ref[...], precise))
    g_ref[...] = _dot(_sigmoid(lora), g2_ref[...], precise)

    cos = jnp.concatenate([cos_ref[...]] * (cw // LANES), axis=1)
    sin = jnp.concatenate([sin_ref[...]] * (cw // LANES), axis=1)
    lane = lax.broadcasted_iota(jnp.int32, (tm, cw), 1)
    lo_half = (lane % HEAD_DIM) < (HEAD_DIM // 2)

    def rope(t):
        partner = jnp.where(lo_half, pltpu.roll(t, cw - HEAD_DIM // 2, axis=1),
                            pltpu.roll(t, HEAD_DIM // 2, axis=1))
        return t * cos + partner * sin

    q_ref[...] = rope(proj[:, cp:cp + cw])
    ka_ref[...] = rope(proj[:, cp + cw:cp + 2 * cw])
    va_ref[...] = proj[:, cp + 2 * cw:cp + 3 * cw]


def _inproj(x2d, sprev, pos_tab, wts, *, precise, seq_len, tm):
    n, d = x2d.shape
    cw, cp = wts["cw"], wts["cp"]
    flat = tm % seq_len == 0 and tm >= seq_len
    nt = n // tm
    tiles_per_seq = None if flat else seq_len // tm
    cos_tab, sin_tab = pos_tab
    ntab = cos_tab.shape[0] // tm
    wdt = F32 if precise else BF16

    const = lambda shape: pl.BlockSpec(shape, lambda i: (0,) * len(shape))
    tok = lambda c: pl.BlockSpec((tm, c), lambda i: (i, 0))
    if flat:
        sprev_spec = pl.BlockSpec((tm, cp), lambda i: (i, 0))
        plast_shape = jax.ShapeDtypeStruct((n, cp), F32)
        plast_spec = pl.BlockSpec((tm, cp), lambda i: (i, 0))
    else:
        sprev_spec = pl.BlockSpec((1, 1, cp), lambda i: (i // tiles_per_seq, 0, 0))
        plast_shape = jax.ShapeDtypeStruct((n // seq_len, 1, cp), F32)
        plast_spec = pl.BlockSpec((1, 1, cp), lambda i: (i // tiles_per_seq, 0, 0))
    in_specs = [
        tok(d), const((1, d)), const((d, cp + 3 * cw)), const((1, cp)), sprev_spec,
        pl.BlockSpec((tm, LANES), lambda i: (i % ntab, 0)),
        pl.BlockSpec((tm, LANES), lambda i: (i % ntab, 0)),
        const((1, cw)), const((LORA_PAD, cw)), const((1, cw)), const((LORA_PAD, cw)),
        const((LORA_PAD, cw)),
    ]
    out_shape = [jax.ShapeDtypeStruct((n, cw), F32)] * 9 + [plast_shape]
    out_specs = [tok(cw)] * 9 + [plast_spec]
    kern = functools.partial(_inproj_kernel, precise=precise, seq_len=seq_len,
                             tiles_per_seq=tiles_per_seq, cw=cw, cp=cp)
    return pl.pallas_call(
        kern, grid=(nt,), in_specs=in_specs, out_specs=out_specs, out_shape=out_shape,
        scratch_shapes=[pltpu.VMEM((1, cp), F32)],
        compiler_params=pltpu.CompilerParams(dimension_semantics=("arbitrary",),
                                             vmem_limit_bytes=VMEM_LIMIT),
        name="inproj_precise" if precise else "inproj",
    )(x2d, wts["norm_mix"], wts["w_cat"].astype(wdt), wts["mu_cat"], sprev, cos_tab, sin_tab,
      wts["w0"], wts["w2p"].astype(wdt), wts["a0"], wts["a2p"].astype(wdt), wts["g2p"].astype(wdt))


def _lane_fold(x, kpar):
    s = LANES // 2
    while s >= LANES // kpar:
        x = x + pltpu.roll(x, s, axis=x.ndim - 1)
        s //= 2
    return x


def _wkv_kernel(r_ref, k_ref, w_ref, a_ref, v_ref, kk_ref, ka_ref, rk_ref, lnw_ref, lnb_ref, s0_ref,
                y_ref, s_ref, aop_ref, wr_ref, bop_ref, km_ref, sc_ref, *, kpar):
    tt, nr, _ = r_ref.shape
    nv = v_ref.shape[1]
    nvb = nv // SUBLANES

    @pl.when(pl.program_id(1) == 0)
    def _():
        s_ref[...] = s0_ref[...]

    r = r_ref[...]
    k = k_ref[...]
    a = a_ref[...]
    kkr = k * kk_ref[...]
    ss = _lane_fold(jnp.sum(kkr * kkr, axis=1, keepdims=True), kpar)
    kkn = kkr / jnp.maximum(jnp.sqrt(ss), 1e-12)
    bop = kkn * a
    km = k * (1.0 + (a - 1.0) * ka_ref[...])
    aop_ref[...] = -kkn
    bop_ref[...] = bop
    km_ref[...] = km
    wr_ref[...] = w_ref[...] * r
    br = _lane_fold(jnp.sum(bop * r, axis=1, keepdims=True), kpar)
    kr = _lane_fold(jnp.sum(km * r, axis=1, keepdims=True), kpar)
    bonus = _lane_fold(jnp.sum(r * km * rk_ref[...], axis=1, keepdims=True), kpar)
    sc_ref[:, 0:1, :] = br
    sc_ref[:, 1:2, :] = kr

    def step(t, carry):
        vv = [v_ref[t, vb * SUBLANES:(vb + 1) * SUBLANES, :] for vb in range(nvb)]
        acc_sa = [None] * nvb
        acc_y = [None] * nvb
        for kr_i in range(nr):
            a_row = aop_ref[t, kr_i:kr_i + 1, :]
            wr_row = wr_ref[t, kr_i:kr_i + 1, :]
            for vb in range(nvb):
                blk = s_ref[kr_i, vb * SUBLANES:(vb + 1) * SUBLANES, :]
                pa = blk * a_row
                py = blk * wr_row
                acc_sa[vb] = pa if acc_sa[vb] is None else acc_sa[vb] + pa
                acc_y[vb] = py if acc_y[vb] is None else acc_y[vb] + py
        sa = [_lane_fold(x, kpar) for x in acc_sa]
        yp = [_lane_fold(x, kpar) for x in acc_y]
        for kr_i in range(nr):
            w_row = w_ref[t, kr_i:kr_i + 1, :]
            b_row = bop_ref[t, kr_i:kr_i + 1, :]
            k_row = km_ref[t, kr_i:kr_i + 1, :]
            for vb in range(nvb):
                sl = slice(vb * SUBLANES, (vb + 1) * SUBLANES)
                s_ref[kr_i, sl, :] = s_ref[kr_i, sl, :] * w_row + sa[vb] * b_row + vv[vb] * k_row
        br_row = sc_ref[t, 0:1, :]
        kr_row = sc_ref[t, 1:2, :]
        for vb in range(nvb):
            y_ref[t, vb * SUBLANES:(vb + 1) * SUBLANES, :] = yp[vb] + sa[vb] * br_row + vv[vb] * kr_row
        return carry

    lax.fori_loop(0, tt, step, 0)

    y = y_ref[...]
    mean = jnp.mean(y, axis=1, keepdims=True)
    yc = y - mean
    var = jnp.mean(yc * yc, axis=1, keepdims=True)
    y_ref[...] = yc * lax.rsqrt(var + LNX_EPS) * lnw_ref[...] + lnb_ref[...] + bonus * v_ref[...]


def _to_lane_layout(x, b, t, h):
    return jnp.transpose(x.reshape(b, t, h, HEAD_DIM), (1, 3, 0, 2)).reshape(t, HEAD_DIM, b * h)


def _wkv(r, k, v, w, a, state, prm, *, b, t, h, tt):
    p = b * h
    if p >= LANES:
        assert p % LANES == 0
        g, kpar = p // LANES, 1
    else:
        assert LANES % p == 0
        g, kpar = 1, LANES // p
    nr = HEAD_DIM // kpar

    def krows(x):
        if kpar > 1:
            return x.reshape(1, x.shape[0], nr, LANES)
        return jnp.transpose(x.reshape(x.shape[0], HEAD_DIM, g, LANES), (2, 0, 1, 3))

    def vrows(x):
        if kpar > 1:
            return jnp.concatenate([x] * kpar, axis=-1)[None]
        return jnp.transpose(x.reshape(x.shape[0], HEAD_DIM, g, LANES), (2, 0, 1, 3))

    rl, kl, wl, al = (krows(_to_lane_layout(z, b, t, h)) for z in (r, k, w, a))
    vl = vrows(_to_lane_layout(v, b, t, h))

    def prm_k(z):
        x = jnp.broadcast_to(z.T[:, None, :], (HEAD_DIM, b, h)).reshape(1, HEAD_DIM, p)
        return krows(x)[:, 0]

    def prm_v(z):
        x = jnp.broadcast_to(z.T[:, None, :], (HEAD_DIM, b, h)).reshape(1, HEAD_DIM, p)
        return vrows(x)[:, 0]

    st = jnp.transpose(state.reshape(p, HEAD_DIM, HEAD_DIM), (2, 1, 0))
    if kpar > 1:
        st = jnp.transpose(st.reshape(nr, kpar, HEAD_DIM, p), (0, 2, 1, 3)).reshape(1, nr, HEAD_DIM, LANES)
    else:
        st = jnp.transpose(st.reshape(HEAD_DIM, HEAD_DIM, g, LANES), (2, 0, 1, 3))

    nt = t // tt
    tile_k = pl.BlockSpec((None, tt, nr, LANES), lambda gi, ti: (gi, ti, 0, 0))
    tile_v = pl.BlockSpec((None, tt, HEAD_DIM, LANES), lambda gi, ti: (gi, ti, 0, 0))
    par_k = pl.BlockSpec((None, nr, LANES), lambda gi, ti: (gi, 0, 0))
    par_v = pl.BlockSpec((None, HEAD_DIM, LANES), lambda gi, ti: (gi, 0, 0))
    st_spec = pl.BlockSpec((None, nr, HEAD_DIM, LANES), lambda gi, ti: (gi, 0, 0, 0))
    y, s_fin = pl.pallas_call(
        functools.partial(_wkv_kernel, kpar=kpar),
        grid=(g, nt),
        in_specs=[tile_k, tile_k, tile_k, tile_k, tile_v, par_k, par_k, par_k, par_v, par_v, st_spec],
        out_specs=[tile_v, st_spec],
        out_shape=[jax.ShapeDtypeStruct((g, t, HEAD_DIM, LANES), F32),
                   jax.ShapeDtypeStruct((g, nr, HEAD_DIM, LANES), F32)],
        scratch_shapes=[pltpu.VMEM((tt, nr, LANES), F32)] * 4 + [pltpu.VMEM((tt, SUBLANES, LANES), F32)],
        compiler_params=pltpu.CompilerParams(dimension_semantics=("arbitrary", "arbitrary"),
                                             vmem_limit_bytes=VMEM_LIMIT),
        name="wkv",
    )(rl, kl, wl, al, vl, prm_k(prm["k_k"]), prm_k(prm["k_a"]), prm_k(prm["r_k"]),
      prm_v(prm["lnx_w"]), prm_v(prm["lnx_b"]), st)

    if kpar > 1:
        yt = y[0, :, :, :p]
        sf = jnp.transpose(s_fin[0].reshape(nr, HEAD_DIM, kpar, p), (3, 1, 0, 2))
    else:
        yt = jnp.transpose(y, (1, 2, 0, 3)).reshape(t, HEAD_DIM, p)
        sf = jnp.transpose(s_fin, (0, 3, 2, 1)).reshape(p, HEAD_DIM, HEAD_DIM)
    y_tok = jnp.transpose(yt.reshape(t, HEAD_DIM, b, h), (2, 0, 3, 1)).reshape(b * t, h * HEAD_DIM)
    return y_tok, sf.reshape(b, h, HEAD_DIM, HEAD_DIM)


def _attn_prompt_kernel(q_ref, k_ref, v_ref, o_ref, ob_ref, lb_ref):
    s_len = q_ref.shape[0]
    blk = 128
    lane = lax.broadcasted_iota(jnp.int32, (blk, LANES), 1)
    head0 = lane < HEAD_DIM
    qi = lax.broadcasted_iota(jnp.int32, (blk, blk), 0)
    kj = lax.broadcasted_iota(jnp.int32, (blk, blk), 1)
    mask_prev = kj >= qi
    mask_own = kj <= qi
    scale = 1.0 / math.sqrt(HEAD_DIM)

    for bi, (window, dil) in enumerate(DILATED):
        assert window // dil == blk
        unit = blk * dil

        def body(idx, carry, bi=bi, dil=dil, unit=unit):
            n = idx // dil
            c = idx % dil
            start = n * unit + c
            pstart = jnp.where(n == 0, start, start - unit)
            no_prev = jnp.where(n == 0, NEG_INF, 0.0)
            rows = pl.ds(start, blk, stride=dil) if dil > 1 else pl.ds(start, blk)
            prows = pl.ds(pstart, blk, stride=dil) if dil > 1 else pl.ds(pstart, blk)
            q = q_ref[rows, :]
            k_own = k_ref[rows, :].astype(BF16)
            v_own = v_ref[rows, :].astype(BF16)
            k_prev = k_ref[prows, :].astype(BF16)
            v_prev = v_ref[prows, :].astype(BF16)
            o_h, l_h = [], []
            for hsel in (head0, jnp.logical_not(head0)):
                qh = jnp.where(hsel, q, 0.0).astype(BF16)
                sp = _dot_nt(qh, k_prev, False) * scale
                so = _dot_nt(qh, k_own, False) * scale
                sp = jnp.where(mask_prev, sp + no_prev, NEG_INF)
                so = jnp.where(mask_own, so, NEG_INF)
                m = jnp.maximum(jnp.max(sp, axis=1, keepdims=True), jnp.max(so, axis=1, keepdims=True))
                ep = jnp.exp(sp - m)
                eo = jnp.exp(so - m)
                l = jnp.sum(ep, axis=1, keepdims=True) + jnp.sum(eo, axis=1, keepdims=True)
                o = _dot(ep, v_prev, False) + _dot(eo, v_own, False)
                o_h.append(o / l)
                l_h.append(m + jnp.log(l))
            ob_ref[bi, rows, :] = jnp.where(head0, o_h[0], o_h[1])
            lb_ref[bi, rows, :] = jnp.where(head0, l_h[0], l_h[1])
            return carry

        lax.fori_loop(0, s_len // blk, body, 0)

    def merge(ci, carry):
        sl = pl.ds(pl.multiple_of(ci * blk, blk), blk)
        l0, l1, l2 = lb_ref[0, sl, :], lb_ref[1, sl, :], lb_ref[2, sl, :]
        m = jnp.maximum(jnp.maximum(l0, l1), l2)
        w0, w1, w2 = jnp.exp(l0 - m), jnp.exp(l1 - m), jnp.exp(l2 - m)
        tot = w0 + w1 + w2
        o_ref[sl, :] = (w0 * ob_ref[0, sl, :] + w1 * ob_ref[1, sl, :] + w2 * ob_ref[2, sl, :]) / tot
        return carry

    lax.fori_loop(0, s_len // blk, merge, 0)


def _attn_prompt(q, k, v, *, b, s):
    cw = q.shape[1]
    hp = cw // LANES
    assert s % (128 * DILATED[-1][1]) == 0
    spec = pl.BlockSpec((None, s, LANES), lambda bi, hi: (bi, 0, hi))
    out = pl.pallas_call(
        _attn_prompt_kernel, grid=(b, hp),
        in_specs=[spec, spec, spec], out_specs=spec,
        out_shape=jax.ShapeDtypeStruct((b, s, cw), F32),
        scratch_shapes=[pltpu.VMEM((3, s, LANES), F32), pltpu.VMEM((3, s, LANES), F32)],
        compiler_params=pltpu.CompilerParams(dimension_semantics=("arbitrary", "arbitrary"),
                                             vmem_limit_bytes=VMEM_LIMIT),
        name="attn_prompt",
    )(q.reshape(b, s, cw), k.reshape(b, s, cw), v.reshape(b, s, cw))
    return out.reshape(b * s, cw)


def _attn_decode_kernel(q_ref, kn_ref, vn_ref, kc_ref, vc_ref, o_ref, *, t_new):
    wc, cw = kc_ref.shape
    nh = cw // HEAD_DIM
    npad = kn_ref.shape[0]
    rows = nh * t_new
    q = q_ref[...]
    qt = jnp.concatenate([q] * nh, axis=0)
    rh = _div_pow2(lax.broadcasted_iota(jnp.int32, (rows, cw), 0), t_new)
    lh = _div_pow2(lax.broadcasted_iota(jnp.int32, (rows, cw), 1), HEAD_DIM)
    own_head = rh == lh
    qbd = jnp.where(own_head, qt, 0.0)
    scale = 1.0 / math.sqrt(HEAD_DIM)
    s_c = _dot_nt(qbd, kc_ref[...], True) * scale
    s_n = _dot_nt(qbd, kn_ref[...], True) * scale
    t_c = _mod_pow2(lax.broadcasted_iota(jnp.int32, (rows, wc), 0), t_new)
    dist_c = wc + t_c - lax.broadcasted_iota(jnp.int32, (rows, wc), 1)
    t_n = _mod_pow2(lax.broadcasted_iota(jnp.int32, (rows, npad), 0), t_new)
    j_n = lax.broadcasted_iota(jnp.int32, (rows, npad), 1)
    dist_n = t_n - j_n
    outs, lses = [], []
    for window, dil in DILATED:
        ok_c = jnp.logical_and(_mod_pow2(dist_c, dil) == 0, dist_c <= window)
        ok_n = jnp.logical_and(jnp.logical_and(dist_n >= 0, j_n < t_new),
                               jnp.logical_and(_mod_pow2(dist_n, dil) == 0, dist_n <= window))
        sc = jnp.where(ok_c, s_c, NEG_INF)
        sn = jnp.where(ok_n, s_n, NEG_INF)
        m = jnp.maximum(jnp.max(sc, axis=1, keepdims=True), jnp.max(sn, axis=1, keepdims=True))
        ec = jnp.exp(sc - m)
        en = jnp.exp(sn - m)
        l = jnp.sum(ec, axis=1, keepdims=True) + jnp.sum(en, axis=1, keepdims=True)
        o = _dot(ec, vc_ref[...], True) + _dot(en, vn_ref[...], True)
        outs.append(o / l)
        lses.append(m + jnp.log(l))
    m = jnp.maximum(jnp.maximum(lses[0], lses[1]), lses[2])
    ws = [jnp.exp(x - m) for x in lses]
    full = (ws[0] * outs[0] + ws[1] * outs[1] + ws[2] * outs[2]) / (ws[0] + ws[1] + ws[2])
    full = jnp.where(own_head, full, 0.0).reshape(nh, t_new, cw)
    o_ref[...] = jnp.sum(full, axis=0)


def _attn_decode(q, k, v, cache_k, cache_v, *, b, t):
    cw = q.shape[1]
    wc = cache_k.shape[1]
    npad = LANES
    pad = lambda z: jnp.pad(z.reshape(b, t, cw), ((0, 0), (0, npad - t), (0, 0)))
    new_spec = pl.BlockSpec((None, npad, cw), lambda bi: (bi, 0, 0))
    cache_spec = pl.BlockSpec((None, wc, cw), lambda bi: (bi, 0, 0))
    tok_spec = pl.BlockSpec((None, t, cw), lambda bi: (bi, 0, 0))
    out = pl.pallas_call(
        functools.partial(_attn_decode_kernel, t_new=t), grid=(b,),
        in_specs=[tok_spec, new_spec, new_spec, cache_spec, cache_spec], out_specs=tok_spec,
        out_shape=jax.ShapeDtypeStruct((b, t, cw), F32),
        compiler_params=pltpu.CompilerParams(dimension_semantics=("arbitrary",),
                                             vmem_limit_bytes=VMEM_LIMIT),
        name="attn_decode",
    )(q.reshape(b, t, cw), pad(k), pad(v), cache_k, cache_v)
    return out.reshape(b * t, cw)


def _route(logits):
    lane_i = lax.broadcasted_iota(jnp.int32, logits.shape, 1)
    lane = lane_i.astype(F32)
    is_g = lane_i < N_GROUPS
    lg = jnp.where(is_g, logits, NEG_INF)
    mg = jnp.max(lg, axis=1, keepdims=True)
    pg_sel = 1.0 / jnp.sum(jnp.where(is_g, jnp.exp(lg - mg), 0.0), axis=1, keepdims=True)
    gsel = jnp.min(jnp.where(lg == mg, lane, float(LANES)), axis=1, keepdims=True)
    e_lane = lane_i - N_GROUPS
    e_group = _div_pow2(e_lane, EXPERTS_PER_GROUP).astype(F32)
    in_sel = jnp.logical_and(jnp.logical_and(e_lane >= 0, e_lane < N_EXPERTS), e_group == gsel)
    le = jnp.where(in_sel, logits, NEG_INF)
    me = jnp.max(le, axis=1, keepdims=True)
    ee = jnp.where(in_sel, jnp.exp(le - me), 0.0)
    pe = jnp.where(in_sel, ee / jnp.sum(ee, axis=1, keepdims=True), -1.0)
    p1 = jnp.max(pe, axis=1, keepdims=True)
    i1 = jnp.min(jnp.where(pe == p1, lane, float(LANES)), axis=1, keepdims=True)
    pe2 = jnp.where(lane == i1, -1.0, pe)
    p2 = jnp.max(pe2, axis=1, keepdims=True)
    i2 = jnp.min(jnp.where(pe2 == p2, lane, float(LANES)), axis=1, keepdims=True)
    tot = p1 + p2
    return jnp.where(lane == i1, pg_sel * p1 / tot, 0.0) + jnp.where(lane == i2, pg_sel * p2 / tot, 0.0)


def _post_kernel(x_ref, rw_ref, g_ref, att_ref, p_ref, an_ref, wo_ref, nf_ref, wr_ref, br_ref,
                 wgu_ref, wd_ref, np_ref, wpg_ref, wpp_ref, nfin_ref,
                 y_ref, h1_ref, u2_ref, gate_ref, acc_ref, *, precise):
    e = pl.program_id(1)
    cw = rw_ref.shape[1]
    ff = wd_ref.shape[0]

    @pl.when(e == 0)
    def _():
        mix_r = rw_ref[...] * g_ref[...]
        mix_a = _rms(att_ref[...], an_ref[...])
        h1 = x_ref[...] + _dot(mix_r, wo_ref[0:cw, :], precise) + _dot(mix_a, wo_ref[cw:2 * cw, :], precise)
        h1_ref[...] = h1
        u2 = _rms(h1, nf_ref[...])
        u2_ref[...] = u2.astype(u2_ref.dtype)
        gate_ref[...] = _route(_dot(u2, wr_ref[...], True) + br_ref[...])
        acc_ref[...] = jnp.zeros_like(acc_ref)

    lane = lax.broadcasted_iota(jnp.int32, gate_ref.shape, 1)
    ge = jnp.sum(jnp.where(lane == e + N_GROUPS, gate_ref[...], 0.0), axis=1, keepdims=True)
    gu = _dot(u2_ref[...], wgu_ref[...], precise)
    gp, up = gu[:, :ff], gu[:, ff:]
    hg = gp * _sigmoid(gp) * up * ge
    acc_ref[...] += _dot(hg, wd_ref[...], precise)

    @pl.when(e == pl.num_programs(1) - 1)
    def _():
        h2 = h1_ref[...] + acc_ref[...]
        u3 = _rms(h2, np_ref[...])
        h3 = h2 + _sigmoid(_dot(u3, wpg_ref[...], precise)) * _dot(p_ref[...], wpp_ref[...], precise)
        y_ref[...] = _rms(h3, nfin_ref[...])


def _post(x2d, rw, g, att, p2d, wts, *, precise, tm):
    n, d = x2d.shape
    cw = rw.shape[1]
    pd = p2d.shape[1]
    ne, _, ff2 = wts["w_gu"].shape
    ff = ff2 // 2
    wdt = F32 if precise else BF16
    tok = lambda c: pl.BlockSpec((tm, c), lambda i, e: (i, 0))
    const = lambda shape: pl.BlockSpec(shape, lambda i, e: (0,) * len(shape))
    in_specs = [
        tok(d), tok(cw), tok(cw), tok(cw), tok(pd),
        const((1, cw)), const((2 * cw, d)), const((1, d)), const((d, LANES)), const((1, LANES)),
        pl.BlockSpec((None, d, ff2), lambda i, e: (e, 0, 0)),
        pl.BlockSpec((None, ff, d), lambda i, e: (e, 0, 0)),
        const((1, d)), const((d, d)), const((pd, d)), const((1, d)),
    ]
    return pl.pallas_call(
        functools.partial(_post_kernel, precise=precise),
        grid=(n // tm, ne), in_specs=in_specs, out_specs=tok(d),
        out_shape=jax.ShapeDtypeStruct((n, d), F32),
        scratch_shapes=[pltpu.VMEM((tm, d), F32), pltpu.VMEM((tm, d), wdt),
                        pltpu.VMEM((tm, LANES), F32), pltpu.VMEM((tm, d), F32)],
        compiler_params=pltpu.CompilerParams(dimension_semantics=("arbitrary", "arbitrary"),
                                             vmem_limit_bytes=VMEM_LIMIT),
        name="post_precise" if precise else "post",
    )(x2d, rw, g, att, p2d, wts["attn_out_norm"], wts["w_out"].astype(wdt), wts["norm_ffn"],
      wts["w_router"], wts["b_router"], wts["w_gu"].astype(wdt), wts["w_d"].astype(wdt),
      wts["norm_ple"], wts["ple_gate_w"].astype(wdt), wts["ple_proj"].astype(wdt), wts["norm_final"])


def _rope_tables(pos):
    half = HEAD_DIM // 2
    inv = jnp.power(ROPE_THETA, -jnp.arange(half, dtype=F32) * 2.0 / HEAD_DIM)
    ang = pos[:, None] * inv[None, :]
    cos, sin = jnp.cos(ang), jnp.sin(ang)
    cos_h = jnp.concatenate([cos, cos], axis=1)
    sin_h = jnp.concatenate([-sin, sin], axis=1)
    reps = LANES // HEAD_DIM
    return jnp.concatenate([cos_h] * reps, axis=1), jnp.concatenate([sin_h] * reps, axis=1)


def _layer_weights(i, norm_mix, w_in, mu_shift, decay_w0, decay_w2, iclr_a0, iclr_a2, gate_g2,
                   k_k, k_a, r_k, lnx_w, lnx_b, attn_out_norm, w_out, norm_ffn,
                   router_group_w, router_group_b, router_expert_w, router_expert_b,
                   expert_w_gate, expert_w_up, expert_w_down, norm_ple, ple_gate_w, ple_proj, norm_final):
    d = w_in.shape[1]
    cw = decay_w0.shape[1]
    n_lora = DECAY_LORA + AAA_LORA + GATE_LORA
    rp = 3 * cw + n_lora
    cp = 3 * cw + LORA_PAD
    padc = lambda z: jnp.pad(z, ((0, 0), (0, LORA_PAD - n_lora)))
    w = w_in[i]
    w_cat = jnp.concatenate([padc(w[:, :rp]), w[:, rp:]], axis=1)
    padr = lambda z, lo: jnp.pad(z, ((lo, LORA_PAD - lo - z.shape[0]), (0, 0)))
    n_route = N_GROUPS + N_EXPERTS
    heads = cw // HEAD_DIM
    return dict(
        cw=cw, cp=cp, rp=rp,
        norm_mix=norm_mix[i][None], w_cat=w_cat, mu_cat=padc(mu_shift[i][None]),
        w0=decay_w0[i][None], w2p=padr(decay_w2[i], 0), a0=iclr_a0[i][None],
        a2p=padr(iclr_a2[i], DECAY_LORA), g2p=padr(gate_g2[i], DECAY_LORA + AAA_LORA),
        k_k=k_k[i].reshape(heads, HEAD_DIM), k_a=k_a[i].reshape(heads, HEAD_DIM), r_k=r_k[i],
        lnx_w=lnx_w[i].reshape(heads, HEAD_DIM), lnx_b=lnx_b[i].reshape(heads, HEAD_DIM),
        attn_out_norm=attn_out_norm[i][None], w_out=w_out[i], norm_ffn=norm_ffn[i][None],
        w_router=jnp.pad(jnp.concatenate([router_group_w[i], router_expert_w[i]], axis=1),
                         ((0, 0), (0, LANES - n_route))),
        b_router=jnp.pad(jnp.concatenate([router_group_b[i], router_expert_b[i]])[None],
                         ((0, 0), (0, LANES - n_route))),
        w_gu=jnp.concatenate([expert_w_gate[i], expert_w_up[i]], axis=2), w_d=expert_w_down[i],
        norm_ple=norm_ple[i][None], ple_gate_w=ple_gate_w[i], ple_proj=ple_proj[i],
        norm_final=norm_final[None],
    )


def _unpad_shift(pl_rows, wts):
    return pl_rows[:, :wts["rp"]]


def kernel(x_prompt, x_sample, cache_k_win, cache_v_win, state_wkv, state_shift, p_prompt, p_sample, norm_mix, w_in, mu_shift, decay_w0, decay_w2, iclr_a0, iclr_a2, gate_g2, k_k, k_a, r_k, lnx_w, lnx_b, attn_out_norm, w_out, norm_ffn, router_group_w, router_group_b, router_expert_w, router_expert_b, expert_w_gate, expert_w_up, expert_w_down, norm_ple, ple_gate_w, ple_proj, norm_final):
    b, s, d = x_prompt.shape
    db, t, _ = x_sample.shape
    depth = w_in.shape[0]
    assert depth == 1, "a deeper stack would chain the layer below over h"
    wts = _layer_weights(0, norm_mix, w_in, mu_shift, decay_w0, decay_w2, iclr_a0, iclr_a2, gate_g2,
                         k_k, k_a, r_k, lnx_w, lnx_b, attn_out_norm, w_out, norm_ffn,
                         router_group_w, router_group_b, router_expert_w, router_expert_b,
                         expert_w_gate, expert_w_up, expert_w_down, norm_ple, ple_gate_w, ple_proj,
                         norm_final)
    cw, cp, rp = wts["cw"], wts["cp"], wts["rp"]
    heads = cw // HEAD_DIM
    keep = min(WIN_MAX, s)
    padc = lambda z: jnp.pad(z, ((0, 0), (0, cp - rp)))

    tm_p = min(256, s)
    pos_p = _rope_tables(jnp.arange(s, dtype=F32))
    zero_shift = jnp.zeros((b, 1, cp), F32)
    r, k, v, w, a, g, q, ka, va, plast = _inproj(
        x_prompt.reshape(b * s, d), zero_shift, pos_p, wts, precise=False, seq_len=s, tm=tm_p)
    zero_wkv = jnp.zeros((b, heads, HEAD_DIM, HEAD_DIM), F32)
    rw, prompt_wkv = _wkv(r, k, v, w, a, zero_wkv, wts, b=b, t=s, h=heads, tt=min(64, s))
    att = _attn_prompt(q, ka, va, b=b, s=s)
    y_prompt = _post(x_prompt.reshape(b * s, d), rw, g, att, p_prompt[0].reshape(b * s, -1), wts,
                     precise=False, tm=min(512, b * s)).reshape(b, s, d)
    prompt_k_win = ka.reshape(b, s, heads, HEAD_DIM)[:, s - keep:][None]
    prompt_v_win = va.reshape(b, s, heads, HEAD_DIM)[:, s - keep:][None]
    prompt_shift = _unpad_shift(plast.reshape(b, cp), wts)[None]

    n_s = db * t
    pos_s = _rope_tables(jnp.tile(PAST_LEN + jnp.arange(t, dtype=F32), db))
    sprev = jnp.repeat(padc(state_shift[0]), t, axis=0)
    r, k, v, w, a, g, q, ka, va, pfull = _inproj(
        x_sample.reshape(n_s, d), sprev, pos_s, wts, precise=True, seq_len=t, tm=n_s)
    rw, sample_wkv = _wkv(r, k, v, w, a, state_wkv[0], wts, b=db, t=t, h=heads, tt=t)
    wc = cache_k_win.shape[2]
    att = _attn_decode(q, ka, va, cache_k_win[0].reshape(db, wc, cw), cache_v_win[0].reshape(db, wc, cw),
                       b=db, t=t)
    y_sample = _post(x_sample.reshape(n_s, d), rw, g, att, p_sample[0].reshape(n_s, -1), wts,
                     precise=True, tm=n_s).reshape(db, t, d)
    sample_k_rows = ka.reshape(db, t, heads, HEAD_DIM)[None]
    sample_v_rows = va.reshape(db, t, heads, HEAD_DIM)[None]
    sample_shift = _unpad_shift(pfull.reshape(db, t, cp)[:, t - 1], wts)[None]

    return (y_prompt, y_sample, prompt_k_win, prompt_v_win, prompt_wkv[None], prompt_shift,
            sample_k_rows, sample_v_rows, sample_wkv[None], sample_shift)
```

```python
import functools
import math

import jax
import jax.numpy as jnp
import numpy as np
from jax import lax
from jax.experimental import pallas as pl
from jax.experimental.pallas import tpu as pltpu

F32 = jnp.float32
BF16 = jnp.bfloat16

HEAD_DIM = 64
DECAY_LORA = 32
AAA_LORA = 32
GATE_LORA = 96
LORA_PAD = 256
DILATED = ((128, 1), (512, 4), (2048, 16))
WIN_MAX = 2048
ROPE_THETA = 10000.0
N_GROUPS = 4
EXPERTS_PER_GROUP = 8
N_EXPERTS = N_GROUPS * EXPERTS_PER_GROUP
PAST_LEN = 16384
RMS_EPS = 1e-6
LNX_EPS = 1e-5 * HEAD_DIM
NEG_INF = -1e30
LANES = 128
SUBLANES = 8
VMEM_LIMIT = 52 * 1024 * 1024
ATTN_UNROLL = 4


def _dot(a, b, precise):
    if precise:
        return jnp.dot(a, b, preferred_element_type=F32, precision=lax.Precision.HIGHEST)
    return jnp.dot(a.astype(BF16), b.astype(BF16), preferred_element_type=F32)


def _dot_nt(a, b, precise):
    dn = (((1,), (1,)), ((), ()))
    if precise:
        return lax.dot_general(a, b, dn, preferred_element_type=F32, precision=lax.Precision.HIGHEST)
    return lax.dot_general(a.astype(BF16), b.astype(BF16), dn, preferred_element_type=F32)


def _div_pow2(x, n):
    assert n & (n - 1) == 0
    return lax.shift_right_arithmetic(x, jnp.int32(n.bit_length() - 1))


def _mod_pow2(x, n):
    assert n & (n - 1) == 0
    return jnp.bitwise_and(x, n - 1)


def _sigmoid(x):
    return 1.0 / (1.0 + jnp.exp(-x))


def _rms(x, g):
    return x * lax.rsqrt(jnp.mean(x * x, axis=-1, keepdims=True) + RMS_EPS) * g


def _inproj_kernel(x_ref, nw_ref, w_ref, mu_ref, sprev_ref, cos_ref, sin_ref,
                   w0_ref, w2_ref, a0_ref, a2_ref, g2_ref,
                   r_ref, k_ref, v_ref, dec_ref, a_ref, g_ref, q_ref, ka_ref, va_ref, plast_ref,
                   carry_ref, *, precise, seq_len, tiles_per_seq, cw, cp):
    tm = x_ref.shape[0]
    u = _rms(x_ref[...], nw_ref[...])
    proj = _dot(u, w_ref[...], precise)
    pc = proj[:, :cp]
    row = lax.broadcasted_iota(jnp.int32, (tm, 1), 0)
    prev = pltpu.roll(pc, 1, axis=0)
    if tiles_per_seq is None:
        prev = jnp.where(_mod_pow2(row, seq_len) == 0, sprev_ref[...], prev)
        plast_ref[...] = pc
    else:
        j = pl.program_id(0) % tiles_per_seq
        first = jnp.where(j == 0, sprev_ref[0], carry_ref[...])
        prev = jnp.where(row == 0, first, prev)
        carry_ref[...] = pc[tm - 1:tm, :]
        plast_ref[0] = pc[tm - 1:tm, :]
    pm = pc + (prev - pc) * mu_ref[...]
    r_ref[...] = pm[:, 0:cw]
    k_ref[...] = pm[:, cw:2 * cw]
    v_ref[...] = pm[:, 2 * cw:3 * cw]
    lora = pm[:, 3 * cw:cp]
    zw = w0_ref[...] + _dot(jnp.tanh(lora), w2_ref[...], precise)
    nz = -zw
    softplus = jnp.maximum(nz, 0.0) + jnp.log(1.0 + jnp.exp(-jnp.abs(nz)))
    dec_ref[...] = jnp.exp(-jnp.exp(-softplus - 0.5))
    a_ref[...] = _sigmoid(a0_ref[...] + _dot(lora, a2_ref[...], precise))
    g_ref[...] = _dot(_sigmoid(lora), g2_ref[...], precise)

    cos = jnp.concatenate([cos_ref[...]] * (cw // LANES), axis=1)
    sin = jnp.concatenate([sin_ref[...]] * (cw // LANES), axis=1)
    lane = lax.broadcasted_iota(jnp.int32, (tm, cw), 1)
    lo_half = (lane % HEAD_DIM) < (HEAD_DIM // 2)

    def rope(t):
        partner = jnp.where(lo_half, pltpu.roll(t, cw - HEAD_DIM // 2, axis=1),
                            pltpu.roll(t, HEAD_DIM // 2, axis=1))
        return t * cos + partner * sin

    q_ref[...] = rope(proj[:, cp:cp + cw])
    ka_ref[...] = rope(proj[:, cp + cw:cp + 2 * cw])
    va_ref[...] = proj[:, cp + 2 * cw:cp + 3 * cw]


def _inproj(x2d, sprev, pos_tab, wts, *, precise, seq_len, tm):
    n, d = x2d.shape
    cw, cp = wts["cw"], wts["cp"]
    flat = tm % seq_len == 0 and tm >= seq_len
    nt = n // tm
    tiles_per_seq = None if flat else seq_len // tm
    cos_tab, sin_tab = pos_tab
    ntab = cos_tab.shape[0] // tm
    wdt = F32 if precise else BF16

    const = lambda shape: pl.BlockSpec(shape, lambda i: (0,) * len(shape))
    tok = lambda c: pl.BlockSpec((tm, c), lambda i: (i, 0))
    if flat:
        sprev_spec = pl.BlockSpec((tm, cp), lambda i: (i, 0))
        plast_shape = jax.ShapeDtypeStruct((n, cp), F32)
        plast_spec = pl.BlockSpec((tm, cp), lambda i: (i, 0))
    else:
        sprev_spec = pl.BlockSpec((1, 1, cp), lambda i: (i // tiles_per_seq, 0, 0))
        plast_shape = jax.ShapeDtypeStruct((n // seq_len, 1, cp), F32)
        plast_spec = pl.BlockSpec((1, 1, cp), lambda i: (i // tiles_per_seq, 0, 0))
    in_specs = [
        tok(d), const((1, d)), const((d, cp + 3 * cw)), const((1, cp)), sprev_spec,
        pl.BlockSpec((tm, LANES), lambda i: (i % ntab, 0)),
        pl.BlockSpec((tm, LANES), lambda i: (i % ntab, 0)),
        const((1, cw)), const((LORA_PAD, cw)), const((1, cw)), const((LORA_PAD, cw)),
        const((LORA_PAD, cw)),
    ]
    out_shape = [jax.ShapeDtypeStruct((n, cw), F32)] * 9 + [plast_shape]
    out_specs = [tok(cw)] * 9 + [plast_spec]
    kern = functools.partial(_inproj_kernel, precise=precise, seq_len=seq_len,
                             tiles_per_seq=tiles_per_seq, cw=cw, cp=cp)
    return pl.pallas_call(
        kern, grid=(nt,), in_specs=in_specs, out_specs=out_specs, out_shape=out_shape,
        scratch_shapes=[pltpu.VMEM((1, cp), F32)],
        compiler_params=pltpu.CompilerParams(dimension_semantics=("arbitrary",),
                                             vmem_limit_bytes=VMEM_LIMIT),
        name="inproj_precise" if precise else "inproj",
    )(x2d, wts["norm_mix"], wts["w_cat"].astype(wdt), wts["mu_cat"], sprev, cos_tab, sin_tab,
      wts["w0"], wts["w2p"].astype(wdt), wts["a0"], wts["a2p"].astype(wdt), wts["g2p"].astype(wdt))


def _lane_fold(x, kpar):
    s = LANES // 2
    while s >= LANES // kpar:
        x = x + pltpu.roll(x, s, axis=x.ndim - 1)
        s //= 2
    return x


def _wkv_kernel(r_ref, k_ref, w_ref, a_ref, v_ref, kk_ref, ka_ref, rk_ref, lnw_ref, lnb_ref, s0_ref,
                y_ref, s_ref, aop_ref, wr_ref, bop_ref, km_ref, sc_ref, *, kpar):
    tt, nr, _ = r_ref.shape
    nv = v_ref.shape[1]
    nvb = nv // SUBLANES

    @pl.when(pl.program_id(1) == 0)
    def _():
        s_ref[...] = s0_ref[...]

    r = r_ref[...]
    k = k_ref[...]
    a = a_ref[...]
    kkr = k * kk_ref[...]
    ss = _lane_fold(jnp.sum(kkr * kkr, axis=1, keepdims=True), kpar)
    kkn = kkr / jnp.maximum(jnp.sqrt(ss), 1e-12)
    bop = kkn * a
    km = k * (1.0 + (a - 1.0) * ka_ref[...])
    aop_ref[...] = -kkn
    bop_ref[...] = bop
    km_ref[...] = km
    wr_ref[...] = w_ref[...] * r
    br = _lane_fold(jnp.sum(bop * r, axis=1, keepdims=True), kpar)
    kr = _lane_fold(jnp.sum(km * r, axis=1, keepdims=True), kpar)
    bonus = _lane_fold(jnp.sum(r * km * rk_ref[...], axis=1, keepdims=True), kpar)
    sc_ref[:, 0:1, :] = br
    sc_ref[:, 1:2, :] = kr

    def step(t, carry):
        vv = [v_ref[t, vb * SUBLANES:(vb + 1) * SUBLANES, :] for vb in range(nvb)]
        acc_sa = [None] * nvb
        acc_y = [None] * nvb
        for kr_i in range(nr):
            a_row = aop_ref[t, kr_i:kr_i + 1, :]
            wr_row = wr_ref[t, kr_i:kr_i + 1, :]
            for vb in range(nvb):
                blk = s_ref[kr_i, vb * SUBLANES:(vb + 1) * SUBLANES, :]
                pa = blk * a_row
                py = blk * wr_row
                acc_sa[vb] = pa if acc_sa[vb] is None else acc_sa[vb] + pa
                acc_y[vb] = py if acc_y[vb] is None else acc_y[vb] + py
        sa = [_lane_fold(x, kpar) for x in acc_sa]
        yp = [_lane_fold(x, kpar) for x in acc_y]
        for kr_i in range(nr):
            w_row = w_ref[t, kr_i:kr_i + 1, :]
            b_row = bop_ref[t, kr_i:kr_i + 1, :]
            k_row = km_ref[t, kr_i:kr_i + 1, :]
            for vb in range(nvb):
                sl = slice(vb * SUBLANES, (vb + 1) * SUBLANES)
                s_ref[kr_i, sl, :] = s_ref[kr_i, sl, :] * w_row + sa[vb] * b_row + vv[vb] * k_row
        br_row = sc_ref[t, 0:1, :]
        kr_row = sc_ref[t, 1:2, :]
        for vb in range(nvb):
            y_ref[t, vb * SUBLANES:(vb + 1) * SUBLANES, :] = yp[vb] + sa[vb] * br_row + vv[vb] * kr_row
        return carry

    lax.fori_loop(0, tt, step, 0)

    y = y_ref[...]
    mean = jnp.mean(y, axis=1, keepdims=True)
    yc = y - mean
    var = jnp.mean(yc * yc, axis=1, keepdims=True)
    y_ref[...] = yc * lax.rsqrt(var + LNX_EPS) * lnw_ref[...] + lnb_ref[...] + bonus * v_ref[...]


def _to_lane_layout(x, b, t, h):
    return jnp.transpose(x.reshape(b, t, h, HEAD_DIM), (1, 3, 0, 2)).reshape(t, HEAD_DIM, b * h)


def _wkv(r, k, v, w, a, state, prm, *, b, t, h, tt):
    p = b * h
    if p >= LANES:
        assert p % LANES == 0
        g, kpar = p // LANES, 1
    else:
        assert LANES % p == 0
        g, kpar = 1, LANES // p
    nr = HEAD_DIM // kpar

    def krows(x):
        if kpar > 1:
            return x.reshape(1, x.shape[0], nr, LANES)
        return jnp.transpose(x.reshape(x.shape[0], HEAD_DIM, g, LANES), (2, 0, 1, 3))

    def vrows(x):
        if kpar > 1:
            return jnp.concatenate([x] * kpar, axis=-1)[None]
        return jnp.transpose(x.reshape(x.shape[0], HEAD_DIM, g, LANES), (2, 0, 1, 3))

    rl, kl, wl, al = (krows(_to_lane_layout(z, b, t, h)) for z in (r, k, w, a))
    vl = vrows(_to_lane_layout(v, b, t, h))

    def prm_k(z):
        x = jnp.broadcast_to(z.T[:, None, :], (HEAD_DIM, b, h)).reshape(1, HEAD_DIM, p)
        return krows(x)[:, 0]

    def prm_v(z):
        x = jnp.broadcast_to(z.T[:, None, :], (HEAD_DIM, b, h)).reshape(1, HEAD_DIM, p)
        return vrows(x)[:, 0]

    st = jnp.transpose(state.reshape(p, HEAD_DIM, HEAD_DIM), (2, 1, 0))
    if kpar > 1:
        st = jnp.transpose(st.reshape(nr, kpar, HEAD_DIM, p), (0, 2, 1, 3)).reshape(1, nr, HEAD_DIM, LANES)
    else:
        st = jnp.transpose(st.reshape(HEAD_DIM, HEAD_DIM, g, LANES), (2, 0, 1, 3))

    nt = t // tt
    tile_k = pl.BlockSpec((None, tt, nr, LANES), lambda gi, ti: (gi, ti, 0, 0))
    tile_v = pl.BlockSpec((None, tt, HEAD_DIM, LANES), lambda gi, ti: (gi, ti, 0, 0))
    par_k = pl.BlockSpec((None, nr, LANES), lambda gi, ti: (gi, 0, 0))
    par_v = pl.BlockSpec((None, HEAD_DIM, LANES), lambda gi, ti: (gi, 0, 0))
    st_spec = pl.BlockSpec((None, nr, HEAD_DIM, LANES), lambda gi, ti: (gi, 0, 0, 0))
    y, s_fin = pl.pallas_call(
        functools.partial(_wkv_kernel, kpar=kpar),
        grid=(g, nt),
        in_specs=[tile_k, tile_k, tile_k, tile_k, tile_v, par_k, par_k, par_k, par_v, par_v, st_spec],
        out_specs=[tile_v, st_spec],
        out_shape=[jax.ShapeDtypeStruct((g, t, HEAD_DIM, LANES), F32),
                   jax.ShapeDtypeStruct((g, nr, HEAD_DIM, LANES), F32)],
        scratch_shapes=[pltpu.VMEM((tt, nr, LANES), F32)] * 4 + [pltpu.VMEM((tt, SUBLANES, LANES), F32)],
        compiler_params=pltpu.CompilerParams(dimension_semantics=("arbitrary", "arbitrary"),
                                             vmem_limit_bytes=VMEM_LIMIT),
        name="wkv",
    )(rl, kl, wl, al, vl, prm_k(prm["k_k"]), prm_k(prm["k_a"]), prm_k(prm["r_k"]),
      prm_v(prm["lnx_w"]), prm_v(prm["lnx_b"]), st)

    if kpar > 1:
        yt = y[0, :, :, :p]
        sf = jnp.transpose(s_fin[0].reshape(nr, HEAD_DIM, kpar, p), (3, 1, 0, 2))
    else:
        yt = jnp.transpose(y, (1, 2, 0, 3)).reshape(t, HEAD_DIM, p)
        sf = jnp.transpose(s_fin, (0, 3, 2, 1)).reshape(p, HEAD_DIM, HEAD_DIM)
    y_tok = jnp.transpose(yt.reshape(t, HEAD_DIM, b, h), (2, 0, 3, 1)).reshape(b * t, h * HEAD_DIM)
    return y_tok, sf.reshape(b, h, HEAD_DIM, HEAD_DIM)


def _attn_prompt_kernel(q_ref, k_ref, v_ref, o_ref, ob_ref, lb_ref, qs_ref, vs_ref, kt_ref,
                        bias2_ref, bias1_ref, *, unroll):
    s_len = q_ref.shape[0]
    blk = 128
    nchunk = s_len // blk
    lane = lax.broadcasted_iota(jnp.int32, (blk, LANES), 1)
    head0 = lane < HEAD_DIM
    head1 = jnp.logical_not(head0)
    qi = _mod_pow2(lax.broadcasted_iota(jnp.int32, (2 * blk, 2 * blk), 0), blk)
    kj = lax.broadcasted_iota(jnp.int32, (2 * blk, 2 * blk), 1)
    bias2_ref[...] = jnp.where(kj < blk, jnp.where(kj >= qi, 0.0, NEG_INF),
                               jnp.where(kj - blk <= qi, 0.0, NEG_INF))
    qi1 = _mod_pow2(lax.broadcasted_iota(jnp.int32, (2 * blk, blk), 0), blk)
    kj1 = lax.broadcasted_iota(jnp.int32, (2 * blk, blk), 1)
    bias1_ref[...] = jnp.where(kj1 <= qi1, 0.0, NEG_INF)
    scale = 1.0 / math.sqrt(HEAD_DIM)

    def finish(bi, rows, s, v0, v1):
        s0, s1 = s[:blk], s[blk:]
        m0 = jnp.max(s0, axis=1, keepdims=True)
        m1 = jnp.max(s1, axis=1, keepdims=True)
        a0 = jnp.dot(jnp.exp(s0 - m0).astype(BF16), v0, preferred_element_type=F32)
        a1 = jnp.dot(jnp.exp(s1 - m1).astype(BF16), v1, preferred_element_type=F32)
        num = jnp.where(head0, a0, a1)
        den = pltpu.roll(jnp.where(head0, a1, a0), HEAD_DIM, axis=1)
        ob_ref[bi, rows, :] = num / den
        lb_ref[bi, rows, :] = jnp.where(head0, m0, m1) + jnp.log(den)

    for bi, (window, dil) in enumerate(DILATED):
        assert window // dil == blk
        unit = blk * dil
        nblk = s_len // unit

        def pos_rows(j, dil=dil, unit=unit, nblk=nblk):
            if dil == 1:
                return pl.ds(pl.multiple_of(j * blk, blk), blk)
            return pl.ds((j % nblk) * unit + j // nblk, blk, stride=dil)

        def stage(j, carry, pos_rows=pos_rows):
            rows = pos_rows(j)
            dst = pl.ds(pl.multiple_of(j * blk, blk), blk)
            q = q_ref[rows, :] * scale
            v = v_ref[rows, :]
            qs_ref[0, dst, :] = jnp.where(head0, q, 0.0).astype(BF16)
            qs_ref[1, dst, :] = jnp.where(head1, q, 0.0).astype(BF16)
            vs_ref[0, dst, :] = jnp.where(head0, v, 1.0).astype(BF16)
            vs_ref[1, dst, :] = jnp.where(head1, v, 1.0).astype(BF16)
            kt_ref[:, dst] = k_ref[rows, :].T.astype(BF16)
            return carry

        def first(c, carry, bi=bi, nblk=nblk, pos_rows=pos_rows):
            j = c * nblk
            src = pl.ds(pl.multiple_of(j * blk, blk), blk)
            qs = jnp.concatenate([qs_ref[0, src, :], qs_ref[1, src, :]], axis=0)
            s = jnp.dot(qs, kt_ref[:, src], preferred_element_type=F32) + bias1_ref[...]
            finish(bi, pos_rows(j), s, vs_ref[0, src, :], vs_ref[1, src, :])
            return carry

        def later(idx, carry, bi=bi, nblk=nblk, pos_rows=pos_rows):
            j = (idx // (nblk - 1)) * nblk + 1 + idx % (nblk - 1)
            src = pl.ds(pl.multiple_of(j * blk, blk), blk)
            keys = pl.ds(pl.multiple_of((j - 1) * blk, blk), 2 * blk)
            qs = jnp.concatenate([qs_ref[0, src, :], qs_ref[1, src, :]], axis=0)
            s = jnp.dot(qs, kt_ref[:, keys], preferred_element_type=F32) + bias2_ref[...]
            finish(bi, pos_rows(j), s, vs_ref[0, keys, :], vs_ref[1, keys, :])
            return carry

        lax.fori_loop(0, nchunk, stage, 0, unroll=2)
        lax.fori_loop(0, dil, first, 0, unroll=min(dil, unroll))
        if nblk > 1:
            lax.fori_loop(0, dil * (nblk - 1), later, 0, unroll=unroll)

    def merge(ci, carry):
        sl = pl.ds(pl.multiple_of(ci * blk, blk), blk)
        l0, l1, l2 = lb_ref[0, sl, :], lb_ref[1, sl, :], lb_ref[2, sl, :]
        m = jnp.maximum(jnp.maximum(l0, l1), l2)
        w0, w1, w2 = jnp.exp(l0 - m), jnp.exp(l1 - m), jnp.exp(l2 - m)
        tot = w0 + w1 + w2
        o_ref[sl, :] = (w0 * ob_ref[0, sl, :] + w1 * ob_ref[1, sl, :] + w2 * ob_ref[2, sl, :]) / tot
        return carry

    lax.fori_loop(0, s_len // blk, merge, 0)


def _attn_prompt(q, k, v, *, b, s):
    cw = q.shape[1]
    hp = cw // LANES
    assert s % (128 * DILATED[-1][1]) == 0
    spec = pl.BlockSpec((None, s, LANES), lambda bi, hi: (bi, 0, hi))
    out = pl.pallas_call(
        functools.partial(_attn_prompt_kernel, unroll=ATTN_UNROLL), grid=(b, hp),
        in_specs=[spec, spec, spec], out_specs=spec,
        out_shape=jax.ShapeDtypeStruct((b, s, cw), F32),
        scratch_shapes=[pltpu.VMEM((3, s, LANES), F32), pltpu.VMEM((3, s, LANES), F32),
                        pltpu.VMEM((2, s, LANES), BF16), pltpu.VMEM((2, s, LANES), BF16),
                        pltpu.VMEM((LANES, s), BF16),
                        pltpu.VMEM((256, 256), F32), pltpu.VMEM((256, 128), F32)],
        compiler_params=pltpu.CompilerParams(dimension_semantics=("arbitrary", "arbitrary"),
                                             vmem_limit_bytes=VMEM_LIMIT),
        name="attn_prompt",
    )(q.reshape(b, s, cw), k.reshape(b, s, cw), v.reshape(b, s, cw))
    return out.reshape(b * s, cw)


def _attn_decode_kernel(q_ref, kn_ref, vn_ref, kc_ref, vc_ref, o_ref, *, t_new):
    wc, cw = kc_ref.shape
    nh = cw // HEAD_DIM
    npad = kn_ref.shape[0]
    rows = nh * t_new
    q = q_ref[...]
    qt = jnp.concatenate([q] * nh, axis=0)
    rh = _div_pow2(lax.broadcasted_iota(jnp.int32, (rows, cw), 0), t_new)
    lh = _div_pow2(lax.broadcasted_iota(jnp.int32, (rows, cw), 1), HEAD_DIM)
    own_head = rh == lh
    qbd = jnp.where(own_head, qt, 0.0)
    scale = 1.0 / math.sqrt(HEAD_DIM)
    s_c = _dot_nt(qbd, kc_ref[...], True) * scale
    s_n = _dot_nt(qbd, kn_ref[...], True) * scale
    t_c = _mod_pow2(lax.broadcasted_iota(jnp.int32, (rows, wc), 0), t_new)
    dist_c = wc + t_c - lax.broadcasted_iota(jnp.int32, (rows, wc), 1)
    t_n = _mod_pow2(lax.broadcasted_iota(jnp.int32, (rows, npad), 0), t_new)
    j_n = lax.broadcasted_iota(jnp.int32, (rows, npad), 1)
    dist_n = t_n - j_n
    ecs, ens, ls, lses = [], [], [], []
    for window, dil in DILATED:
        ok_c = jnp.logical_and(_mod_pow2(dist_c, dil) == 0, dist_c <= window)
        ok_n = jnp.logical_and(jnp.logical_and(dist_n >= 0, j_n < t_new),
                               jnp.logical_and(_mod_pow2(dist_n, dil) == 0, dist_n <= window))
        sc = jnp.where(ok_c, s_c, NEG_INF)
        sn = jnp.where(ok_n, s_n, NEG_INF)
        m = jnp.maximum(jnp.max(sc, axis=1, keepdims=True), jnp.max(sn, axis=1, keepdims=True))
        ec = jnp.exp(sc - m)
        en = jnp.exp(sn - m)
        l = jnp.sum(ec, axis=1, keepdims=True) + jnp.sum(en, axis=1, keepdims=True)
        ecs.append(ec)
        ens.append(en)
        ls.append(l)
        lses.append(m + jnp.log(l))
    o_all = (_dot(jnp.concatenate(ecs, axis=0), vc_ref[...], True)
             + _dot(jnp.concatenate(ens, axis=0), vn_ref[...], True))
    outs = [o_all[i * rows:(i + 1) * rows] / ls[i] for i in range(len(DILATED))]
    m = jnp.maximum(jnp.maximum(lses[0], lses[1]), lses[2])
    ws = [jnp.exp(x - m) for x in lses]
    full = (ws[0] * outs[0] + ws[1] * outs[1] + ws[2] * outs[2]) / (ws[0] + ws[1] + ws[2])
    full = jnp.where(own_head, full, 0.0).reshape(nh, t_new, cw)
    o_ref[...] = jnp.sum(full, axis=0)


def _attn_decode(q, k, v, cache_k, cache_v, *, b, t):
    cw = q.shape[1]
    wc = cache_k.shape[1]
    npad = LANES
    pad = lambda z: jnp.pad(z.reshape(b, t, cw), ((0, 0), (0, npad - t), (0, 0)))
    new_spec = pl.BlockSpec((None, npad, cw), lambda bi: (bi, 0, 0))
    cache_spec = pl.BlockSpec((None, wc, cw), lambda bi: (bi, 0, 0))
    tok_spec = pl.BlockSpec((None, t, cw), lambda bi: (bi, 0, 0))
    out = pl.pallas_call(
        functools.partial(_attn_decode_kernel, t_new=t), grid=(b,),
        in_specs=[tok_spec, new_spec, new_spec, cache_spec, cache_spec], out_specs=tok_spec,
        out_shape=jax.ShapeDtypeStruct((b, t, cw), F32),
        compiler_params=pltpu.CompilerParams(dimension_semantics=("arbitrary",),
                                             vmem_limit_bytes=VMEM_LIMIT),
        name="attn_decode",
    )(q.reshape(b, t, cw), pad(k), pad(v), cache_k, cache_v)
    return out.reshape(b * t, cw)


def _route(logits):
    lane_i = lax.broadcasted_iota(jnp.int32, logits.shape, 1)
    lane = lane_i.astype(F32)
    is_g = lane_i < N_GROUPS
    lg = jnp.where(is_g, logits, NEG_INF)
    mg = jnp.max(lg, axis=1, keepdims=True)
    pg_sel = 1.0 / jnp.sum(jnp.where(is_g, jnp.exp(lg - mg), 0.0), axis=1, keepdims=True)
    gsel = jnp.min(jnp.where(lg == mg, lane, float(LANES)), axis=1, keepdims=True)
    e_lane = lane_i - N_GROUPS
    e_group = _div_pow2(e_lane, EXPERTS_PER_GROUP).astype(F32)
    in_sel = jnp.logical_and(jnp.logical_and(e_lane >= 0, e_lane < N_EXPERTS), e_group == gsel)
    le = jnp.where(in_sel, logits, NEG_INF)
    me = jnp.max(le, axis=1, keepdims=True)
    ee = jnp.where(in_sel, jnp.exp(le - me), 0.0)
    pe = jnp.where(in_sel, ee / jnp.sum(ee, axis=1, keepdims=True), -1.0)
    p1 = jnp.max(pe, axis=1, keepdims=True)
    i1 = jnp.min(jnp.where(pe == p1, lane, float(LANES)), axis=1, keepdims=True)
    pe2 = jnp.where(lane == i1, -1.0, pe)
    p2 = jnp.max(pe2, axis=1, keepdims=True)
    i2 = jnp.min(jnp.where(pe2 == p2, lane, float(LANES)), axis=1, keepdims=True)
    tot = p1 + p2
    return jnp.where(lane == i1, pg_sel * p1 / tot, 0.0) + jnp.where(lane == i2, pg_sel * p2 / tot, 0.0)


def _post_kernel(x_ref, rw_ref, g_ref, att_ref, p_ref, an_ref, wo_ref, nf_ref, wr_ref, br_ref,
                 wgu_ref, wd_ref, np_ref, wpg_ref, wpp_ref, nfin_ref,
                 y_ref, h1_ref, u2_ref, gate_ref, acc_ref, *, precise):
    e = pl.program_id(1)
    cw = rw_ref.shape[1]
    ff = wd_ref.shape[0]

    @pl.when(e == 0)
    def _():
        mix_r = rw_ref[...] * g_ref[...]
        mix_a = _rms(att_ref[...], an_ref[...])
        h1 = x_ref[...] + _dot(mix_r, wo_ref[0:cw, :], precise) + _dot(mix_a, wo_ref[cw:2 * cw, :], precise)
        h1_ref[...] = h1
        u2 = _rms(h1, nf_ref[...])
        u2_ref[...] = u2.astype(u2_ref.dtype)
        gate_ref[...] = _route(_dot(u2, wr_ref[...], True) + br_ref[...])
        acc_ref[...] = jnp.zeros_like(acc_ref)

    lane = lax.broadcasted_iota(jnp.int32, gate_ref.shape, 1)
    ge = jnp.sum(jnp.where(lane == e + N_GROUPS, gate_ref[...], 0.0), axis=1, keepdims=True)
    gu = _dot(u2_ref[...], wgu_ref[...], precise)
    gp, up = gu[:, :ff], gu[:, ff:]
    hg = gp * _sigmoid(gp) * up * ge
    acc_ref[...] += _dot(hg, wd_ref[...], precise)

    @pl.when(e == pl.num_programs(1) - 1)
    def _():
        h2 = h1_ref[...] + acc_ref[...]
        u3 = _rms(h2, np_ref[...])
        h3 = h2 + _sigmoid(_dot(u3, wpg_ref[...], precise)) * _dot(p_ref[...], wpp_ref[...], precise)
        y_ref[...] = _rms(h3, nfin_ref[...])


def _post(x2d, rw, g, att, p2d, wts, *, precise, tm):
    n, d = x2d.shape
    cw = rw.shape[1]
    pd = p2d.shape[1]
    ne, _, ff2 = wts["w_gu"].shape
    ff = ff2 // 2
    wdt = F32 if precise else BF16
    tok = lambda c: pl.BlockSpec((tm, c), lambda i, e: (i, 0))
    const = lambda shape: pl.BlockSpec(shape, lambda i, e: (0,) * len(shape))
    in_specs = [
        tok(d), tok(cw), tok(cw), tok(cw), tok(pd),
        const((1, cw)), const((2 * cw, d)), const((1, d)), const((d, LANES)), const((1, LANES)),
        pl.BlockSpec((None, d, ff2), lambda i, e: (e, 0, 0)),
        pl.BlockSpec((None, ff, d), lambda i, e: (e, 0, 0)),
        const((1, d)), const((d, d)), const((pd, d)), const((1, d)),
    ]
    return pl.pallas_call(
        functools.partial(_post_kernel, precise=precise),
        grid=(n // tm, ne), in_specs=in_specs, out_specs=tok(d),
        out_shape=jax.ShapeDtypeStruct((n, d), F32),
        scratch_shapes=[pltpu.VMEM((tm, d), F32), pltpu.VMEM((tm, d), wdt),
                        pltpu.VMEM((tm, LANES), F32), pltpu.VMEM((tm, d), F32)],
        compiler_params=pltpu.CompilerParams(dimension_semantics=("arbitrary", "arbitrary"),
                                             vmem_limit_bytes=VMEM_LIMIT),
        name="post_precise" if precise else "post",
    )(x2d, rw, g, att, p2d, wts["attn_out_norm"], wts["w_out"].astype(wdt), wts["norm_ffn"],
      wts["w_router"], wts["b_router"], wts["w_gu"].astype(wdt), wts["w_d"].astype(wdt),
      wts["norm_ple"], wts["ple_gate_w"].astype(wdt), wts["ple_proj"].astype(wdt), wts["norm_final"])


def _rope_tables(pos):
    half = HEAD_DIM // 2
    inv = np.power(ROPE_THETA, -np.arange(half, dtype=np.float64) * 2.0 / HEAD_DIM)
    ang = np.asarray(pos, np.float64)[:, None] * inv[None, :]
    cos, sin = np.cos(ang), np.sin(ang)
    cos_h = np.concatenate([cos, cos], axis=1)
    sin_h = np.concatenate([-sin, sin], axis=1)
    reps = LANES // HEAD_DIM
    return (jnp.asarray(np.concatenate([cos_h] * reps, axis=1), F32),
            jnp.asarray(np.concatenate([sin_h] * reps, axis=1), F32))


def _layer_weights(i, norm_mix, w_in, mu_shift, decay_w0, decay_w2, iclr_a0, iclr_a2, gate_g2,
                   k_k, k_a, r_k, lnx_w, lnx_b, attn_out_norm, w_out, norm_ffn,
                   router_group_w, router_group_b, router_expert_w, router_expert_b,
                   expert_w_gate, expert_w_up, expert_w_down, norm_ple, ple_gate_w, ple_proj, norm_final):
    d = w_in.shape[1]
    cw = decay_w0.shape[1]
    n_lora = DECAY_LORA + AAA_LORA + GATE_LORA
    rp = 3 * cw + n_lora
    cp = 3 * cw + LORA_PAD
    padc = lambda z: jnp.pad(z, ((0, 0), (0, LORA_PAD - n_lora)))
    w = w_in[i]
    w_cat = jnp.concatenate([padc(w[:, :rp]), w[:, rp:]], axis=1)
    padr = lambda z, lo: jnp.pad(z, ((lo, LORA_PAD - lo - z.shape[0]), (0, 0)))
    n_route = N_GROUPS + N_EXPERTS
    heads = cw // HEAD_DIM
    return dict(
        cw=cw, cp=cp, rp=rp,
        norm_mix=norm_mix[i][None], w_cat=w_cat, mu_cat=padc(mu_shift[i][None]),
        w0=decay_w0[i][None], w2p=padr(decay_w2[i], 0), a0=iclr_a0[i][None],
        a2p=padr(iclr_a2[i], DECAY_LORA), g2p=padr(gate_g2[i], DECAY_LORA + AAA_LORA),
        k_k=k_k[i].reshape(heads, HEAD_DIM), k_a=k_a[i].reshape(heads, HEAD_DIM), r_k=r_k[i],
        lnx_w=lnx_w[i].reshape(heads, HEAD_DIM), lnx_b=lnx_b[i].reshape(heads, HEAD_DIM),
        attn_out_norm=attn_out_norm[i][None], w_out=w_out[i], norm_ffn=norm_ffn[i][None],
        w_router=jnp.pad(jnp.concatenate([router_group_w[i], router_expert_w[i]], axis=1),
                         ((0, 0), (0, LANES - n_route))),
        b_router=jnp.pad(jnp.concatenate([router_group_b[i], router_expert_b[i]])[None],
                         ((0, 0), (0, LANES - n_route))),
        w_gu=jnp.concatenate([expert_w_gate[i], expert_w_up[i]], axis=2), w_d=expert_w_down[i],
        norm_ple=norm_ple[i][None], ple_gate_w=ple_gate_w[i], ple_proj=ple_proj[i],
        norm_final=norm_final[None],
    )


def _unpad_shift(pl_rows, wts):
    return pl_rows[:, :wts["rp"]]


def kernel(x_prompt, x_sample, cache_k_win, cache_v_win, state_wkv, state_shift, p_prompt, p_sample, norm_mix, w_in, mu_shift, decay_w0, decay_w2, iclr_a0, iclr_a2, gate_g2, k_k, k_a, r_k, lnx_w, lnx_b, attn_out_norm, w_out, norm_ffn, router_group_w, router_group_b, router_expert_w, router_expert_b, expert_w_gate, expert_w_up, expert_w_down, norm_ple, ple_gate_w, ple_proj, norm_final):
    b, s, d = x_prompt.shape
    db, t, _ = x_sample.shape
    depth = w_in.shape[0]
    assert depth == 1, "a deeper stack would chain the layer below over h"
    wts = _layer_weights(0, norm_mix, w_in, mu_shift, decay_w0, decay_w2, iclr_a0, iclr_a2, gate_g2,
                         k_k, k_a, r_k, lnx_w, lnx_b, attn_out_norm, w_out, norm_ffn,
                         router_group_w, router_group_b, router_expert_w, router_expert_b,
                         expert_w_gate, expert_w_up, expert_w_down, norm_ple, ple_gate_w, ple_proj,
                         norm_final)
    cw, cp, rp = wts["cw"], wts["cp"], wts["rp"]
    heads = cw // HEAD_DIM
    keep = min(WIN_MAX, s)
    padc = lambda z: jnp.pad(z, ((0, 0), (0, cp - rp)))

    tm_p = min(256, s)
    pos_p = _rope_tables(np.arange(s))
    zero_shift = jnp.zeros((b, 1, cp), F32)
    r, k, v, w, a, g, q, ka, va, plast = _inproj(
        x_prompt.reshape(b * s, d), zero_shift, pos_p, wts, precise=False, seq_len=s, tm=tm_p)
    zero_wkv = jnp.zeros((b, heads, HEAD_DIM, HEAD_DIM), F32)
    rw, prompt_wkv = _wkv(r, k, v, w, a, zero_wkv, wts, b=b, t=s, h=heads, tt=min(64, s))
    att = _attn_prompt(q, ka, va, b=b, s=s)
    y_prompt = _post(x_prompt.reshape(b * s, d), rw, g, att, p_prompt[0].reshape(b * s, -1), wts,
                     precise=False, tm=min(512, b * s)).reshape(b, s, d)
    prompt_k_win = ka.reshape(b, s, heads, HEAD_DIM)[:, s - keep:][None]
    prompt_v_win = va.reshape(b, s, heads, HEAD_DIM)[:, s - keep:][None]
    prompt_shift = _unpad_shift(plast.reshape(b, cp), wts)[None]

    n_s = db * t
    pos_s = _rope_tables(np.tile(PAST_LEN + np.arange(t), db))
    sprev = jnp.repeat(padc(state_shift[0]), t, axis=0)
    r, k, v, w, a, g, q, ka, va, pfull = _inproj(
        x_sample.reshape(n_s, d), sprev, pos_s, wts, precise=True, seq_len=t, tm=n_s)
    rw, sample_wkv = _wkv(r, k, v, w, a, state_wkv[0], wts, b=db, t=t, h=heads, tt=t)
    wc = cache_k_win.shape[2]
    att = _attn_decode(q, ka, va, cache_k_win[0].reshape(db, wc, cw), cache_v_win[0].reshape(db, wc, cw),
                       b=db, t=t)
    y_sample = _post(x_sample.reshape(n_s, d), rw, g, att, p_sample[0].reshape(n_s, -1), wts,
                     precise=True, tm=n_s).reshape(db, t, d)
    sample_k_rows = ka.reshape(db, t, heads, HEAD_DIM)[None]
    sample_v_rows = va.reshape(db, t, heads, HEAD_DIM)[None]
    sample_shift = _unpad_shift(pfull.reshape(db, t, cp)[:, t - 1], wts)[None]

    return (y_prompt, y_sample, prompt_k_win, prompt_v_win, prompt_wkv[None], prompt_shift,
            sample_k_rows, sample_v_rows, sample_wkv[None], sample_shift)
```

```python
import functools
import math

import jax
import jax.numpy as jnp
import numpy as np
from jax import lax
from jax.experimental import pallas as pl
from jax.experimental.pallas import tpu as pltpu

F32 = jnp.float32
BF16 = jnp.bfloat16

HEAD_DIM = 64
DECAY_LORA = 32
AAA_LORA = 32
GATE_LORA = 96
LORA_PAD = 256
DILATED = ((128, 1), (512, 4), (2048, 16))
WIN_MAX = 2048
ROPE_THETA = 10000.0
N_GROUPS = 4
EXPERTS_PER_GROUP = 8
N_EXPERTS = N_GROUPS * EXPERTS_PER_GROUP
PAST_LEN = 16384
RMS_EPS = 1e-6
LNX_EPS = 1e-5 * HEAD_DIM
NEG_INF = -1e30
LANES = 128
SUBLANES = 8
VMEM_LIMIT = 52 * 1024 * 1024
ATTN_UNROLL = 4


def _dot(a, b, precise):
    if precise:
        return jnp.dot(a, b, preferred_element_type=F32, precision=lax.Precision.HIGHEST)
    return jnp.dot(a.astype(BF16), b.astype(BF16), preferred_element_type=F32)


def _dot_nt(a, b, precise):
    dn = (((1,), (1,)), ((), ()))
    if precise:
        return lax.dot_general(a, b, dn, preferred_element_type=F32, precision=lax.Precision.HIGHEST)
    return lax.dot_general(a.astype(BF16), b.astype(BF16), dn, preferred_element_type=F32)


def _div_pow2(x, n):
    assert n & (n - 1) == 0
    return lax.shift_right_arithmetic(x, jnp.int32(n.bit_length() - 1))


def _mod_pow2(x, n):
    assert n & (n - 1) == 0
    return jnp.bitwise_and(x, n - 1)


def _sigmoid(x):
    return 1.0 / (1.0 + jnp.exp(-x))


def _rms(x, g):
    return x * lax.rsqrt(jnp.mean(x * x, axis=-1, keepdims=True) + RMS_EPS) * g


def _inproj_kernel(x_ref, nw_ref, w_ref, mu_ref, sprev_ref, cos_ref, sin_ref,
                   w0_ref, w2_ref, a0_ref, a2_ref, g2_ref,
                   r_ref, k_ref, v_ref, dec_ref, a_ref, g_ref, q_ref, ka_ref, va_ref, plast_ref,
                   carry_ref, *, precise, seq_len, tiles_per_seq, cw, cp):
    tm = x_ref.shape[0]
    u = _rms(x_ref[...], nw_ref[...])
    proj = _dot(u, w_ref[...], precise)
    pc = proj[:, :cp]
    row = lax.broadcasted_iota(jnp.int32, (tm, 1), 0)
    prev = pltpu.roll(pc, 1, axis=0)
    if tiles_per_seq is None:
        prev = jnp.where(_mod_pow2(row, seq_len) == 0, sprev_ref[...], prev)
        plast_ref[...] = pc
    else:
        j = pl.program_id(0) % tiles_per_seq
        first = jnp.where(j == 0, sprev_ref[0], carry_ref[...])
        prev = jnp.where(row == 0, first, prev)
        carry_ref[...] = pc[tm - 1:tm, :]
        plast_ref[0] = pc[tm - 1:tm, :]
    pm = pc + (prev - pc) * mu_ref[...]
    r_ref[...] = pm[:, 0:cw]
    k_ref[...] = pm[:, cw:2 * cw]
    v_ref[...] = pm[:, 2 * cw:3 * cw]
    lora = pm[:, 3 * cw:cp]
    zw = w0_ref[...] + _dot(jnp.tanh(lora), w2_ref[...], precise)
    nz = -zw
    softplus = jnp.maximum(nz, 0.0) + jnp.log(1.0 + jnp.exp(-jnp.abs(nz)))
    dec_ref[...] = jnp.exp(-jnp.exp(-softplus - 0.5))
    a_ref[...] = _sigmoid(a0_ref[...] + _dot(lora, a2_ref[...], precise))
    g_ref[...] = _dot(_sigmoid(lora), g2_ref[...], precise)

    cos = jnp.concatenate([cos_ref[...]] * (cw // LANES), axis=1)
    sin = jnp.concatenate([sin_ref[...]] * (cw // LANES), axis=1)
    lane = lax.broadcasted_iota(jnp.int32, (tm, cw), 1)
    lo_half = (lane % HEAD_DIM) < (HEAD_DIM // 2)

    def rope(t):
        partner = jnp.where(lo_half, pltpu.roll(t, cw - HEAD_DIM // 2, axis=1),
                            pltpu.roll(t, HEAD_DIM // 2, axis=1))
        return t * cos + partner * sin

    q_ref[...] = rope(proj[:, cp:cp + cw])
    ka_ref[...] = rope(proj[:, cp + cw:cp + 2 * cw])
    va_ref[...] = proj[:, cp + 2 * cw:cp + 3 * cw]


def _inproj(x2d, sprev, pos_tab, wts, *, precise, seq_len, tm):
    n, d = x2d.shape
    cw, cp = wts["cw"], wts["cp"]
    flat = tm % seq_len == 0 and tm >= seq_len
    nt = n // tm
    tiles_per_seq = None if flat else seq_len // tm
    cos_tab, sin_tab = pos_tab
    ntab = cos_tab.shape[0] // tm
    wdt = F32 if precise else BF16

    const = lambda shape: pl.BlockSpec(shape, lambda i: (0,) * len(shape))
    tok = lambda c: pl.BlockSpec((tm, c), lambda i: (i, 0))
    if flat:
        sprev_spec = pl.BlockSpec((tm, cp), lambda i: (i, 0))
        plast_shape = jax.ShapeDtypeStruct((n, cp), F32)
        plast_spec = pl.BlockSpec((tm, cp), lambda i: (i, 0))
    else:
        sprev_spec = pl.BlockSpec((1, 1, cp), lambda i: (i // tiles_per_seq, 0, 0))
        plast_shape = jax.ShapeDtypeStruct((n // seq_len, 1, cp), F32)
        plast_spec = pl.BlockSpec((1, 1, cp), lambda i: (i // tiles_per_seq, 0, 0))
    in_specs = [
        tok(d), const((1, d)), const((d, cp + 3 * cw)), const((1, cp)), sprev_spec,
        pl.BlockSpec((tm, LANES), lambda i: (i % ntab, 0)),
        pl.BlockSpec((tm, LANES), lambda i: (i % ntab, 0)),
        const((1, cw)), const((LORA_PAD, cw)), const((1, cw)), const((LORA_PAD, cw)),
        const((LORA_PAD, cw)),
    ]
    out_shape = [jax.ShapeDtypeStruct((n, cw), F32)] * 9 + [plast_shape]
    out_specs = [tok(cw)] * 9 + [plast_spec]
    kern = functools.partial(_inproj_kernel, precise=precise, seq_len=seq_len,
                             tiles_per_seq=tiles_per_seq, cw=cw, cp=cp)
    return pl.pallas_call(
        kern, grid=(nt,), in_specs=in_specs, out_specs=out_specs, out_shape=out_shape,
        scratch_shapes=[pltpu.VMEM((1, cp), F32)],
        compiler_params=pltpu.CompilerParams(dimension_semantics=("arbitrary",),
                                             vmem_limit_bytes=VMEM_LIMIT),
        name="inproj_precise" if precise else "inproj",
    )(x2d, wts["norm_mix"], wts["w_cat"].astype(wdt), wts["mu_cat"], sprev, cos_tab, sin_tab,
      wts["w0"], wts["w2p"].astype(wdt), wts["a0"], wts["a2p"].astype(wdt), wts["g2p"].astype(wdt))


def _inproj_cm_kernel(x_ref, nw_ref, wt_ref, wa_ref, mu_ref, sprev_ref, cos_ref, sin_ref,
                      w0_ref, w2_ref, a0_ref, a2_ref, g2_ref,
                      r_ref, k_ref, v_ref, dec_ref, a_ref, g_ref, q_ref, ka_ref, va_ref, plast_ref,
                      carry_ref, *, tiles_per_seq, cw, cp):
    tm = x_ref.shape[0]
    nh = cw // HEAD_DIM
    u = _rms(x_ref[...], nw_ref[...]).astype(BF16)
    pt = _dot_nt(wt_ref[...], u, False)
    j = pl.program_id(0) % tiles_per_seq
    before = jnp.where(j == 0, sprev_ref[0], carry_ref[...])
    lane = lax.broadcasted_iota(jnp.int32, (cp, LANES), 1)
    prev = pltpu.roll(pt, 1, axis=1)
    head_blk = jnp.where(lane == 0, pltpu.roll(before, 1, axis=1), prev[:, :LANES])
    prev = jnp.concatenate([head_blk, prev[:, LANES:]], axis=1) if tm > LANES else head_blk
    carry_ref[...] = pt[:, tm - LANES:]
    plast_ref[0] = pt[:, tm - LANES:]
    lanes_of = lambda z: jnp.concatenate([z] * (tm // LANES), axis=1)
    pm = pt + (prev - pt) * lanes_of(mu_ref[...])
    cm = lambda z: z.reshape(HEAD_DIM, nh, tm)
    r_ref[...] = cm(pm[0:cw])
    k_ref[...] = cm(pm[cw:2 * cw])
    v_ref[...] = cm(pm[2 * cw:3 * cw])
    lora = pm[3 * cw:cp]
    zw = lanes_of(w0_ref[...]) + _dot(w2_ref[...], jnp.tanh(lora), False)
    nz = -zw
    softplus = jnp.maximum(nz, 0.0) + jnp.log(1.0 + jnp.exp(-jnp.abs(nz)))
    dec_ref[...] = cm(jnp.exp(-jnp.exp(-softplus - 0.5)))
    a_ref[...] = cm(_sigmoid(lanes_of(a0_ref[...]) + _dot(a2_ref[...], lora, False)))
    g_ref[...] = _dot(g2_ref[...], _sigmoid(lora), False).T

    proj = _dot(u, wa_ref[...], False)
    cos = jnp.concatenate([cos_ref[...]] * (cw // LANES), axis=1)
    sin = jnp.concatenate([sin_ref[...]] * (cw // LANES), axis=1)
    lane_c = lax.broadcasted_iota(jnp.int32, (tm, cw), 1)
    lo_half = _mod_pow2(lane_c, HEAD_DIM) < (HEAD_DIM // 2)

    def rope(t):
        partner = jnp.where(lo_half, pltpu.roll(t, cw - HEAD_DIM // 2, axis=1),
                            pltpu.roll(t, HEAD_DIM // 2, axis=1))
        return t * cos + partner * sin

    q_ref[...] = rope(proj[:, 0:cw])
    ka_ref[...] = rope(proj[:, cw:2 * cw])
    va_ref[...] = proj[:, 2 * cw:3 * cw]


def _inproj_cm(x2d, pos_tab, wts, *, b, seq_len, tm):
    n, d = x2d.shape
    cw, cp = wts["cw"], wts["cp"]
    nh = cw // HEAD_DIM
    tiles_per_seq = seq_len // tm
    cos_tab, sin_tab = pos_tab
    const = lambda shape: pl.BlockSpec(shape, lambda i: (0,) * len(shape))
    tok = lambda c: pl.BlockSpec((tm, c), lambda i: (i, 0))
    chan = pl.BlockSpec((HEAD_DIM, nh, tm), lambda i: (0, i // tiles_per_seq, i % tiles_per_seq))
    per_b = pl.BlockSpec((1, cp, LANES), lambda i: (i // tiles_per_seq, 0, 0))
    in_specs = [
        tok(d), const((1, d)), const((cp, d)), const((d, 3 * cw)), const((cp, LANES)), per_b,
        pl.BlockSpec((tm, LANES), lambda i: (i % tiles_per_seq, 0)),
        pl.BlockSpec((tm, LANES), lambda i: (i % tiles_per_seq, 0)),
        const((cw, LANES)), const((cw, LORA_PAD)), const((cw, LANES)), const((cw, LORA_PAD)),
        const((cw, LORA_PAD)),
    ]
    chan_shape = jax.ShapeDtypeStruct((HEAD_DIM, b * nh, seq_len), F32)
    out_shape = [chan_shape] * 5 + [jax.ShapeDtypeStruct((n, cw), F32)] * 4 + [
        jax.ShapeDtypeStruct((b, cp, LANES), F32)]
    out_specs = [chan] * 5 + [tok(cw)] * 4 + [per_b]
    perm = lambda z: jnp.transpose(z.reshape(z.shape[0], -1, nh, HEAD_DIM), (0, 1, 3, 2)).reshape(z.shape)
    rp = wts["rp"]
    w = wts["w_cat"]
    wt = jnp.concatenate([perm(w[:, :3 * cw]), w[:, 3 * cw:cp]], axis=1).T.astype(BF16)
    mu = wts["mu_cat"]
    col = lambda z: jnp.broadcast_to(z.reshape(-1, 1), (z.size, LANES))
    mu_c = col(jnp.concatenate([perm(mu[:, :3 * cw]), mu[:, 3 * cw:]], axis=1))
    zero_shift = jnp.zeros((b, cp, LANES), F32)
    outs = pl.pallas_call(
        functools.partial(_inproj_cm_kernel, tiles_per_seq=tiles_per_seq, cw=cw, cp=cp),
        grid=(n // tm,), in_specs=in_specs, out_specs=out_specs, out_shape=out_shape,
        scratch_shapes=[pltpu.VMEM((cp, LANES), F32)],
        compiler_params=pltpu.CompilerParams(dimension_semantics=("arbitrary",),
                                             vmem_limit_bytes=VMEM_LIMIT),
        name="inproj",
    )(x2d, wts["norm_mix"], wt, w[:, cp:].astype(BF16), mu_c, zero_shift, cos_tab, sin_tab,
      col(perm(wts["w0"])), perm(wts["w2p"]).T.astype(BF16), col(perm(wts["a0"])),
      perm(wts["a2p"]).T.astype(BF16), wts["g2p"].T.astype(BF16))
    last = outs[9][:, :, LANES - 1]
    unperm = lambda z: jnp.transpose(z.reshape(z.shape[0], -1, HEAD_DIM, nh), (0, 1, 3, 2)).reshape(z.shape)
    plast = jnp.concatenate([unperm(last[:, :3 * cw]), last[:, 3 * cw:rp]], axis=1)
    return list(outs[:9]) + [plast]


def _lane_fold(x, kpar):
    s = LANES // 2
    while s >= LANES // kpar:
        x = x + pltpu.roll(x, s, axis=x.ndim - 1)
        s //= 2
    return x


def _wkv_kernel(r_ref, k_ref, w_ref, a_ref, v_ref, kk_ref, ka_ref, rk_ref, lnw_ref, lnb_ref, s0_ref,
                y_ref, s_ref, aop_ref, wr_ref, bop_ref, km_ref, sc_ref, *, kpar):
    tt, nr, _ = r_ref.shape
    nv = v_ref.shape[1]
    nvb = nv // SUBLANES

    @pl.when(pl.program_id(1) == 0)
    def _():
        s_ref[...] = s0_ref[...]

    r = r_ref[...]
    k = k_ref[...]
    a = a_ref[...]
    kkr = k * kk_ref[...]
    ss = _lane_fold(jnp.sum(kkr * kkr, axis=1, keepdims=True), kpar)
    kkn = kkr / jnp.maximum(jnp.sqrt(ss), 1e-12)
    bop = kkn * a
    km = k * (1.0 + (a - 1.0) * ka_ref[...])
    aop_ref[...] = -kkn
    bop_ref[...] = bop
    km_ref[...] = km
    wr_ref[...] = w_ref[...] * r
    br = _lane_fold(jnp.sum(bop * r, axis=1, keepdims=True), kpar)
    kr = _lane_fold(jnp.sum(km * r, axis=1, keepdims=True), kpar)
    bonus = _lane_fold(jnp.sum(r * km * rk_ref[...], axis=1, keepdims=True), kpar)
    sc_ref[:, 0:1, :] = br
    sc_ref[:, 1:2, :] = kr

    def step(t, carry):
        vv = [v_ref[t, vb * SUBLANES:(vb + 1) * SUBLANES, :] for vb in range(nvb)]
        acc_sa = [None] * nvb
        acc_y = [None] * nvb
        for kr_i in range(nr):
            a_row = aop_ref[t, kr_i:kr_i + 1, :]
            wr_row = wr_ref[t, kr_i:kr_i + 1, :]
            for vb in range(nvb):
                blk = s_ref[kr_i, vb * SUBLANES:(vb + 1) * SUBLANES, :]
                pa = blk * a_row
                py = blk * wr_row
                acc_sa[vb] = pa if acc_sa[vb] is None else acc_sa[vb] + pa
                acc_y[vb] = py if acc_y[vb] is None else acc_y[vb] + py
        sa = [_lane_fold(x, kpar) for x in acc_sa]
        yp = [_lane_fold(x, kpar) for x in acc_y]
        for kr_i in range(nr):
            w_row = w_ref[t, kr_i:kr_i + 1, :]
            b_row = bop_ref[t, kr_i:kr_i + 1, :]
            k_row = km_ref[t, kr_i:kr_i + 1, :]
            for vb in range(nvb):
                sl = slice(vb * SUBLANES, (vb + 1) * SUBLANES)
                s_ref[kr_i, sl, :] = s_ref[kr_i, sl, :] * w_row + sa[vb] * b_row + vv[vb] * k_row
        br_row = sc_ref[t, 0:1, :]
        kr_row = sc_ref[t, 1:2, :]
        for vb in range(nvb):
            y_ref[t, vb * SUBLANES:(vb + 1) * SUBLANES, :] = yp[vb] + sa[vb] * br_row + vv[vb] * kr_row
        return carry

    lax.fori_loop(0, tt, step, 0)

    y = y_ref[...]
    mean = jnp.mean(y, axis=1, keepdims=True)
    yc = y - mean
    var = jnp.mean(yc * yc, axis=1, keepdims=True)
    y_ref[...] = yc * lax.rsqrt(var + LNX_EPS) * lnw_ref[...] + lnb_ref[...] + bonus * v_ref[...]


def _to_lane_layout(x, b, t, h):
    return jnp.transpose(x.reshape(b, t, h, HEAD_DIM), (1, 3, 0, 2)).reshape(t, HEAD_DIM, b * h)


def _wkv(r, k, v, w, a, state, prm, *, b, t, h, tt):
    p = b * h
    if p >= LANES:
        assert p % LANES == 0
        g, kpar = p // LANES, 1
    else:
        assert LANES % p == 0
        g, kpar = 1, LANES // p
    nr = HEAD_DIM // kpar

    def krows(x):
        if kpar > 1:
            return x.reshape(1, x.shape[0], nr, LANES)
        return jnp.transpose(x.reshape(x.shape[0], HEAD_DIM, g, LANES), (2, 0, 1, 3))

    def vrows(x):
        if kpar > 1:
            return jnp.concatenate([x] * kpar, axis=-1)[None]
        return jnp.transpose(x.reshape(x.shape[0], HEAD_DIM, g, LANES), (2, 0, 1, 3))

    rl, kl, wl, al = (krows(z) for z in (r, k, w, a))
    vl = vrows(v)

    def prm_k(z):
        x = jnp.broadcast_to(z.T[:, None, :], (HEAD_DIM, b, h)).reshape(1, HEAD_DIM, p)
        return krows(x)[:, 0]

    def prm_v(z):
        x = jnp.broadcast_to(z.T[:, None, :], (HEAD_DIM, b, h)).reshape(1, HEAD_DIM, p)
        return vrows(x)[:, 0]

    st = jnp.transpose(state.reshape(p, HEAD_DIM, HEAD_DIM), (2, 1, 0))
    if kpar > 1:
        st = jnp.transpose(st.reshape(nr, kpar, HEAD_DIM, p), (0, 2, 1, 3)).reshape(1, nr, HEAD_DIM, LANES)
    else:
        st = jnp.transpose(st.reshape(HEAD_DIM, HEAD_DIM, g, LANES), (2, 0, 1, 3))

    nt = t // tt
    tile_k = pl.BlockSpec((None, tt, nr, LANES), lambda gi, ti: (gi, ti, 0, 0))
    tile_v = pl.BlockSpec((None, tt, HEAD_DIM, LANES), lambda gi, ti: (gi, ti, 0, 0))
    par_k = pl.BlockSpec((None, nr, LANES), lambda gi, ti: (gi, 0, 0))
    par_v = pl.BlockSpec((None, HEAD_DIM, LANES), lambda gi, ti: (gi, 0, 0))
    st_spec = pl.BlockSpec((None, nr, HEAD_DIM, LANES), lambda gi, ti: (gi, 0, 0, 0))
    y, s_fin = pl.pallas_call(
        functools.partial(_wkv_kernel, kpar=kpar),
        grid=(g, nt),
        in_specs=[tile_k, tile_k, tile_k, tile_k, tile_v, par_k, par_k, par_k, par_v, par_v, st_spec],
        out_specs=[tile_v, st_spec],
        out_shape=[jax.ShapeDtypeStruct((g, t, HEAD_DIM, LANES), F32),
                   jax.ShapeDtypeStruct((g, nr, HEAD_DIM, LANES), F32)],
        scratch_shapes=[pltpu.VMEM((tt, nr, LANES), F32)] * 4 + [pltpu.VMEM((tt, SUBLANES, LANES), F32)],
        compiler_params=pltpu.CompilerParams(dimension_semantics=("arbitrary", "arbitrary"),
                                             vmem_limit_bytes=VMEM_LIMIT),
        name="wkv",
    )(rl, kl, wl, al, vl, prm_k(prm["k_k"]), prm_k(prm["k_a"]), prm_k(prm["r_k"]),
      prm_v(prm["lnx_w"]), prm_v(prm["lnx_b"]), st)

    if kpar > 1:
        yt = y[0, :, :, :p]
        sf = jnp.transpose(s_fin[0].reshape(nr, HEAD_DIM, kpar, p), (3, 1, 0, 2))
    else:
        yt = jnp.transpose(y, (1, 2, 0, 3)).reshape(t, HEAD_DIM, p)
        sf = jnp.transpose(s_fin, (0, 3, 2, 1)).reshape(p, HEAD_DIM, HEAD_DIM)
    y_tok = jnp.transpose(yt.reshape(t, HEAD_DIM, b, h), (2, 0, 3, 1)).reshape(b * t, h * HEAD_DIM)
    return y_tok, sf.reshape(b, h, HEAD_DIM, HEAD_DIM)


def _attn_prompt_kernel(q_ref, k_ref, v_ref, o_ref, ob_ref, lb_ref, qs_ref, vs_ref, kt_ref,
                        bias2_ref, bias1_ref, *, unroll):
    s_len = q_ref.shape[0]
    blk = 128
    nchunk = s_len // blk
    lane = lax.broadcasted_iota(jnp.int32, (blk, LANES), 1)
    head0 = lane < HEAD_DIM
    head1 = jnp.logical_not(head0)
    qi = _mod_pow2(lax.broadcasted_iota(jnp.int32, (2 * blk, 2 * blk), 0), blk)
    kj = lax.broadcasted_iota(jnp.int32, (2 * blk, 2 * blk), 1)
    bias2_ref[...] = jnp.where(kj < blk, jnp.where(kj >= qi, 0.0, NEG_INF),
                               jnp.where(kj - blk <= qi, 0.0, NEG_INF))
    qi1 = _mod_pow2(lax.broadcasted_iota(jnp.int32, (2 * blk, blk), 0), blk)
    kj1 = lax.broadcasted_iota(jnp.int32, (2 * blk, blk), 1)
    bias1_ref[...] = jnp.where(kj1 <= qi1, 0.0, NEG_INF)
    scale = 1.0 / math.sqrt(HEAD_DIM)

    def finish(bi, rows, s, v0, v1):
        s0, s1 = s[:blk], s[blk:]
        m0 = jnp.max(s0, axis=1, keepdims=True)
        m1 = jnp.max(s1, axis=1, keepdims=True)
        a0 = jnp.dot(jnp.exp(s0 - m0).astype(BF16), v0, preferred_element_type=F32)
        a1 = jnp.dot(jnp.exp(s1 - m1).astype(BF16), v1, preferred_element_type=F32)
        num = jnp.where(head0, a0, a1)
        den = pltpu.roll(jnp.where(head0, a1, a0), HEAD_DIM, axis=1)
        ob_ref[bi, rows, :] = num / den
        lb_ref[bi, rows, :] = jnp.where(head0, m0, m1) + jnp.log(den)

    for bi, (window, dil) in enumerate(DILATED):
        assert window // dil == blk
        unit = blk * dil
        nblk = s_len // unit

        def pos_rows(j, dil=dil, unit=unit, nblk=nblk):
            if dil == 1:
                return pl.ds(pl.multiple_of(j * blk, blk), blk)
            return pl.ds((j % nblk) * unit + j // nblk, blk, stride=dil)

        def stage(j, carry, pos_rows=pos_rows):
            rows = pos_rows(j)
            dst = pl.ds(pl.multiple_of(j * blk, blk), blk)
            q = q_ref[rows, :] * scale
            v = v_ref[rows, :]
            qs_ref[0, dst, :] = jnp.where(head0, q, 0.0).astype(BF16)
            qs_ref[1, dst, :] = jnp.where(head1, q, 0.0).astype(BF16)
            vs_ref[0, dst, :] = jnp.where(head0, v, 1.0).astype(BF16)
            vs_ref[1, dst, :] = jnp.where(head1, v, 1.0).astype(BF16)
            kt_ref[:, dst] = k_ref[rows, :].T.astype(BF16)
            return carry

        def first(c, carry, bi=bi, nblk=nblk, pos_rows=pos_rows):
            j = c * nblk
            src = pl.ds(pl.multiple_of(j * blk, blk), blk)
            qs = jnp.concatenate([qs_ref[0, src, :], qs_ref[1, src, :]], axis=0)
            s = jnp.dot(qs, kt_ref[:, src], preferred_element_type=F32) + bias1_ref[...]
            finish(bi, pos_rows(j), s, vs_ref[0, src, :], vs_ref[1, src, :])
            return carry

        def later(idx, carry, bi=bi, nblk=nblk, pos_rows=pos_rows):
            j = (idx // (nblk - 1)) * nblk + 1 + idx % (nblk - 1)
            src = pl.ds(pl.multiple_of(j * blk, blk), blk)
            keys = pl.ds(pl.multiple_of((j - 1) * blk, blk), 2 * blk)
            qs = jnp.concatenate([qs_ref[0, src, :], qs_ref[1, src, :]], axis=0)
            s = jnp.dot(qs, kt_ref[:, keys], preferred_element_type=F32) + bias2_ref[...]
            finish(bi, pos_rows(j), s, vs_ref[0, keys, :], vs_ref[1, keys, :])
            return carry

        lax.fori_loop(0, nchunk, stage, 0, unroll=2)
        lax.fori_loop(0, dil, first, 0, unroll=min(dil, unroll))
        if nblk > 1:
            lax.fori_loop(0, dil * (nblk - 1), later, 0, unroll=unroll)

    def merge(ci, carry):
        sl = pl.ds(pl.multiple_of(ci * blk, blk), blk)
        l0, l1, l2 = lb_ref[0, sl, :], lb_ref[1, sl, :], lb_ref[2, sl, :]
        m = jnp.maximum(jnp.maximum(l0, l1), l2)
        w0, w1, w2 = jnp.exp(l0 - m), jnp.exp(l1 - m), jnp.exp(l2 - m)
        tot = w0 + w1 + w2
        o_ref[sl, :] = (w0 * ob_ref[0, sl, :] + w1 * ob_ref[1, sl, :] + w2 * ob_ref[2, sl, :]) / tot
        return carry

    lax.fori_loop(0, s_len // blk, merge, 0)


def _attn_prompt(q, k, v, *, b, s):
    cw = q.shape[1]
    hp = cw // LANES
    assert s % (128 * DILATED[-1][1]) == 0
    spec = pl.BlockSpec((None, s, LANES), lambda bi, hi: (bi, 0, hi))
    out = pl.pallas_call(
        functools.partial(_attn_prompt_kernel, unroll=ATTN_UNROLL), grid=(b, hp),
        in_specs=[spec, spec, spec], out_specs=spec,
        out_shape=jax.ShapeDtypeStruct((b, s, cw), F32),
        scratch_shapes=[pltpu.VMEM((3, s, LANES), F32), pltpu.VMEM((3, s, LANES), F32),
                        pltpu.VMEM((2, s, LANES), BF16), pltpu.VMEM((2, s, LANES), BF16),
                        pltpu.VMEM((LANES, s), BF16),
                        pltpu.VMEM((256, 256), F32), pltpu.VMEM((256, 128), F32)],
        compiler_params=pltpu.CompilerParams(dimension_semantics=("arbitrary", "arbitrary"),
                                             vmem_limit_bytes=VMEM_LIMIT),
        name="attn_prompt",
    )(q.reshape(b, s, cw), k.reshape(b, s, cw), v.reshape(b, s, cw))
    return out.reshape(b * s, cw)


def _attn_decode_kernel(q_ref, kn_ref, vn_ref, kc_ref, vc_ref, o_ref, *, t_new):
    wc, cw = kc_ref.shape
    nh = cw // HEAD_DIM
    npad = kn_ref.shape[0]
    rows = nh * t_new
    q = q_ref[...]
    qt = jnp.concatenate([q] * nh, axis=0)
    rh = _div_pow2(lax.broadcasted_iota(jnp.int32, (rows, cw), 0), t_new)
    lh = _div_pow2(lax.broadcasted_iota(jnp.int32, (rows, cw), 1), HEAD_DIM)
    own_head = rh == lh
    qbd = jnp.where(own_head, qt, 0.0)
    scale = 1.0 / math.sqrt(HEAD_DIM)
    s_c = _dot_nt(qbd, kc_ref[...], True) * scale
    s_n = _dot_nt(qbd, kn_ref[...], True) * scale
    t_c = _mod_pow2(lax.broadcasted_iota(jnp.int32, (rows, wc), 0), t_new)
    dist_c = wc + t_c - lax.broadcasted_iota(jnp.int32, (rows, wc), 1)
    t_n = _mod_pow2(lax.broadcasted_iota(jnp.int32, (rows, npad), 0), t_new)
    j_n = lax.broadcasted_iota(jnp.int32, (rows, npad), 1)
    dist_n = t_n - j_n
    ecs, ens, ls, lses = [], [], [], []
    for window, dil in DILATED:
        ok_c = jnp.logical_and(_mod_pow2(dist_c, dil) == 0, dist_c <= window)
        ok_n = jnp.logical_and(jnp.logical_and(dist_n >= 0, j_n < t_new),
                               jnp.logical_and(_mod_pow2(dist_n, dil) == 0, dist_n <= window))
        sc = jnp.where(ok_c, s_c, NEG_INF)
        sn = jnp.where(ok_n, s_n, NEG_INF)
        m = jnp.maximum(jnp.max(sc, axis=1, keepdims=True), jnp.max(sn, axis=1, keepdims=True))
        ec = jnp.exp(sc - m)
        en = jnp.exp(sn - m)
        l = jnp.sum(ec, axis=1, keepdims=True) + jnp.sum(en, axis=1, keepdims=True)
        ecs.append(ec)
        ens.append(en)
        ls.append(l)
        lses.append(m + jnp.log(l))
    o_all = (_dot(jnp.concatenate(ecs, axis=0), vc_ref[...], True)
             + _dot(jnp.concatenate(ens, axis=0), vn_ref[...], True))
    outs = [o_all[i * rows:(i + 1) * rows] / ls[i] for i in range(len(DILATED))]
    m = jnp.maximum(jnp.maximum(lses[0], lses[1]), lses[2])
    ws = [jnp.exp(x - m) for x in lses]
    full = (ws[0] * outs[0] + ws[1] * outs[1] + ws[2] * outs[2]) / (ws[0] + ws[1] + ws[2])
    full = jnp.where(own_head, full, 0.0).reshape(nh, t_new, cw)
    o_ref[...] = jnp.sum(full, axis=0)


def _attn_decode(q, k, v, cache_k, cache_v, *, b, t):
    cw = q.shape[1]
    wc = cache_k.shape[1]
    npad = LANES
    pad = lambda z: jnp.pad(z.reshape(b, t, cw), ((0, 0), (0, npad - t), (0, 0)))
    new_spec = pl.BlockSpec((None, npad, cw), lambda bi: (bi, 0, 0))
    cache_spec = pl.BlockSpec((None, wc, cw), lambda bi: (bi, 0, 0))
    tok_spec = pl.BlockSpec((None, t, cw), lambda bi: (bi, 0, 0))
    out = pl.pallas_call(
        functools.partial(_attn_decode_kernel, t_new=t), grid=(b,),
        in_specs=[tok_spec, new_spec, new_spec, cache_spec, cache_spec], out_specs=tok_spec,
        out_shape=jax.ShapeDtypeStruct((b, t, cw), F32),
        compiler_params=pltpu.CompilerParams(dimension_semantics=("arbitrary",),
                                             vmem_limit_bytes=VMEM_LIMIT),
        name="attn_decode",
    )(q.reshape(b, t, cw), pad(k), pad(v), cache_k, cache_v)
    return out.reshape(b * t, cw)


def _route(logits):
    lane_i = lax.broadcasted_iota(jnp.int32, logits.shape, 1)
    lane = lane_i.astype(F32)
    is_g = lane_i < N_GROUPS
    lg = jnp.where(is_g, logits, NEG_INF)
    mg = jnp.max(lg, axis=1, keepdims=True)
    pg_sel = 1.0 / jnp.sum(jnp.where(is_g, jnp.exp(lg - mg), 0.0), axis=1, keepdims=True)
    gsel = jnp.min(jnp.where(lg == mg, lane, float(LANES)), axis=1, keepdims=True)
    e_lane = lane_i - N_GROUPS
    e_group = _div_pow2(e_lane, EXPERTS_PER_GROUP).astype(F32)
    in_sel = jnp.logical_and(jnp.logical_and(e_lane >= 0, e_lane < N_EXPERTS), e_group == gsel)
    le = jnp.where(in_sel, logits, NEG_INF)
    me = jnp.max(le, axis=1, keepdims=True)
    ee = jnp.where(in_sel, jnp.exp(le - me), 0.0)
    pe = jnp.where(in_sel, ee / jnp.sum(ee, axis=1, keepdims=True), -1.0)
    p1 = jnp.max(pe, axis=1, keepdims=True)
    i1 = jnp.min(jnp.where(pe == p1, lane, float(LANES)), axis=1, keepdims=True)
    pe2 = jnp.where(lane == i1, -1.0, pe)
    p2 = jnp.max(pe2, axis=1, keepdims=True)
    i2 = jnp.min(jnp.where(pe2 == p2, lane, float(LANES)), axis=1, keepdims=True)
    tot = p1 + p2
    return jnp.where(lane == i1, pg_sel * p1 / tot, 0.0) + jnp.where(lane == i2, pg_sel * p2 / tot, 0.0)


def _post_kernel(x_ref, rw_ref, g_ref, att_ref, p_ref, an_ref, wo_ref, nf_ref, wr_ref, br_ref,
                 wg_ref, wu_ref, wd_ref, np_ref, wpg_ref, wpp_ref, nfin_ref,
                 y_ref, h1_ref, u2_ref, gate_ref, acc_ref, *, precise):
    e = pl.program_id(1)
    cw = rw_ref.shape[1]
    ne_step, ff, _ = wd_ref.shape

    @pl.when(e == 0)
    def _():
        mix_r = rw_ref[...] * g_ref[...]
        mix_a = _rms(att_ref[...], an_ref[...])
        h1 = x_ref[...] + _dot(mix_r, wo_ref[0:cw, :], precise) + _dot(mix_a, wo_ref[cw:2 * cw, :], precise)
        h1_ref[...] = h1
        u2 = _rms(h1, nf_ref[...])
        u2_ref[...] = u2.astype(u2_ref.dtype)
        gate_ref[...] = _route(_dot(u2, wr_ref[...], True) + br_ref[...])
        acc_ref[...] = jnp.zeros_like(acc_ref)

    lane = lax.broadcasted_iota(jnp.int32, gate_ref.shape, 1)
    u2 = u2_ref[...]
    gate = gate_ref[...]
    hidden = []
    for i in range(ne_step):
        ge = jnp.sum(jnp.where(lane == e * ne_step + i + N_GROUPS, gate, 0.0), axis=1, keepdims=True)
        gp = _dot(u2, wg_ref[i], precise)
        up = _dot(u2, wu_ref[i], precise)
        hidden.append((gp * _sigmoid(gp) * up * ge).astype(u2.dtype))
    acc_ref[...] += _dot(jnp.concatenate(hidden, axis=1), wd_ref[...].reshape(ne_step * ff, -1), precise)

    @pl.when(e == pl.num_programs(1) - 1)
    def _():
        h2 = h1_ref[...] + acc_ref[...]
        u3 = _rms(h2, np_ref[...])
        h3 = h2 + _sigmoid(_dot(u3, wpg_ref[...], precise)) * _dot(p_ref[...], wpp_ref[...], precise)
        y_ref[...] = _rms(h3, nfin_ref[...])


def _post(x2d, rw, g, att, p2d, wts, *, precise, tm, ne_step):
    n, d = x2d.shape
    cw = rw.shape[1]
    pd = p2d.shape[1]
    ne, _, ff = wts["w_g"].shape
    assert ne % ne_step == 0
    wdt = F32 if precise else BF16
    tok = lambda c: pl.BlockSpec((tm, c), lambda i, e: (i, 0))
    const = lambda shape: pl.BlockSpec(shape, lambda i, e: (0,) * len(shape))
    in_specs = [
        tok(d), tok(cw), tok(cw), tok(cw), tok(pd),
        const((1, cw)), const((2 * cw, d)), const((1, d)), const((d, LANES)), const((1, LANES)),
        pl.BlockSpec((ne_step, d, ff), lambda i, e: (e, 0, 0)),
        pl.BlockSpec((ne_step, d, ff), lambda i, e: (e, 0, 0)),
        pl.BlockSpec((ne_step, ff, d), lambda i, e: (e, 0, 0)),
        const((1, d)), const((d, d)), const((pd, d)), const((1, d)),
    ]
    return pl.pallas_call(
        functools.partial(_post_kernel, precise=precise),
        grid=(n // tm, ne // ne_step), in_specs=in_specs, out_specs=tok(d),
        out_shape=jax.ShapeDtypeStruct((n, d), F32),
        scratch_shapes=[pltpu.VMEM((tm, d), F32), pltpu.VMEM((tm, d), wdt),
                        pltpu.VMEM((tm, LANES), F32), pltpu.VMEM((tm, d), F32)],
        compiler_params=pltpu.CompilerParams(dimension_semantics=("arbitrary", "arbitrary"),
                                             vmem_limit_bytes=VMEM_LIMIT),
        name="post_precise" if precise else "post",
    )(x2d, rw, g, att, p2d, wts["attn_out_norm"], wts["w_out"].astype(wdt), wts["norm_ffn"],
      wts["w_router"], wts["b_router"], wts["w_g"].astype(wdt), wts["w_u"].astype(wdt),
      wts["w_d"].astype(wdt),
      wts["norm_ple"], wts["ple_gate_w"].astype(wdt), wts["ple_proj"].astype(wdt), wts["norm_final"])


def _rope_tables(pos):
    half = HEAD_DIM // 2
    inv = np.power(ROPE_THETA, -np.arange(half, dtype=np.float64) * 2.0 / HEAD_DIM)
    ang = np.asarray(pos, np.float64)[:, None] * inv[None, :]
    cos, sin = np.cos(ang), np.sin(ang)
    cos_h = np.concatenate([cos, cos], axis=1)
    sin_h = np.concatenate([-sin, sin], axis=1)
    reps = LANES // HEAD_DIM
    return (jnp.asarray(np.concatenate([cos_h] * reps, axis=1), F32),
            jnp.asarray(np.concatenate([sin_h] * reps, axis=1), F32))


def _layer_weights(i, norm_mix, w_in, mu_shift, decay_w0, decay_w2, iclr_a0, iclr_a2, gate_g2,
                   k_k, k_a, r_k, lnx_w, lnx_b, attn_out_norm, w_out, norm_ffn,
                   router_group_w, router_group_b, router_expert_w, router_expert_b,
                   expert_w_gate, expert_w_up, expert_w_down, norm_ple, ple_gate_w, ple_proj, norm_final):
    d = w_in.shape[1]
    cw = decay_w0.shape[1]
    n_lora = DECAY_LORA + AAA_LORA + GATE_LORA
    rp = 3 * cw + n_lora
    cp = 3 * cw + LORA_PAD
    padc = lambda z: jnp.pad(z, ((0, 0), (0, LORA_PAD - n_lora)))
    w = w_in[i]
    w_cat = jnp.concatenate([padc(w[:, :rp]), w[:, rp:]], axis=1)
    padr = lambda z, lo: jnp.pad(z, ((lo, LORA_PAD - lo - z.shape[0]), (0, 0)))
    n_route = N_GROUPS + N_EXPERTS
    heads = cw // HEAD_DIM
    return dict(
        cw=cw, cp=cp, rp=rp,
        norm_mix=norm_mix[i][None], w_cat=w_cat, mu_cat=padc(mu_shift[i][None]),
        w0=decay_w0[i][None], w2p=padr(decay_w2[i], 0), a0=iclr_a0[i][None],
        a2p=padr(iclr_a2[i], DECAY_LORA), g2p=padr(gate_g2[i], DECAY_LORA + AAA_LORA),
        k_k=k_k[i].reshape(heads, HEAD_DIM), k_a=k_a[i].reshape(heads, HEAD_DIM), r_k=r_k[i],
        lnx_w=lnx_w[i].reshape(heads, HEAD_DIM), lnx_b=lnx_b[i].reshape(heads, HEAD_DIM),
        attn_out_norm=attn_out_norm[i][None], w_out=w_out[i], norm_ffn=norm_ffn[i][None],
        w_router=jnp.pad(jnp.concatenate([router_group_w[i], router_expert_w[i]], axis=1),
                         ((0, 0), (0, LANES - n_route))),
        b_router=jnp.pad(jnp.concatenate([router_group_b[i], router_expert_b[i]])[None],
                         ((0, 0), (0, LANES - n_route))),
        w_g=expert_w_gate[i], w_u=expert_w_up[i], w_d=expert_w_down[i],
        norm_ple=norm_ple[i][None], ple_gate_w=ple_gate_w[i], ple_proj=ple_proj[i],
        norm_final=norm_final[None],
    )


def _unpad_shift(pl_rows, wts):
    return pl_rows[:, :wts["rp"]]


def kernel(x_prompt, x_sample, cache_k_win, cache_v_win, state_wkv, state_shift, p_prompt, p_sample, norm_mix, w_in, mu_shift, decay_w0, decay_w2, iclr_a0, iclr_a2, gate_g2, k_k, k_a, r_k, lnx_w, lnx_b, attn_out_norm, w_out, norm_ffn, router_group_w, router_group_b, router_expert_w, router_expert_b, expert_w_gate, expert_w_up, expert_w_down, norm_ple, ple_gate_w, ple_proj, norm_final):
    b, s, d = x_prompt.shape
    db, t, _ = x_sample.shape
    depth = w_in.shape[0]
    assert depth == 1, "a deeper stack would chain the layer below over h"
    wts = _layer_weights(0, norm_mix, w_in, mu_shift, decay_w0, decay_w2, iclr_a0, iclr_a2, gate_g2,
                         k_k, k_a, r_k, lnx_w, lnx_b, attn_out_norm, w_out, norm_ffn,
                         router_group_w, router_group_b, router_expert_w, router_expert_b,
                         expert_w_gate, expert_w_up, expert_w_down, norm_ple, ple_gate_w, ple_proj,
                         norm_final)
    cw, cp, rp = wts["cw"], wts["cp"], wts["rp"]
    heads = cw // HEAD_DIM
    keep = min(WIN_MAX, s)
    padc = lambda z: jnp.pad(z, ((0, 0), (0, cp - rp)))

    tm_p = min(256, s)
    pos_p = _rope_tables(np.arange(s))
    r, k, v, w, a, g, q, ka, va, plast = _inproj_cm(
        x_prompt.reshape(b * s, d), pos_p, wts, b=b, seq_len=s, tm=tm_p)
    zero_wkv = jnp.zeros((b, heads, HEAD_DIM, HEAD_DIM), F32)
    lane_p = lambda z: z.reshape(HEAD_DIM * b * heads, s).T.reshape(s, HEAD_DIM, b * heads)
    rw, prompt_wkv = _wkv(lane_p(r), lane_p(k), lane_p(v), lane_p(w), lane_p(a), zero_wkv, wts,
                          b=b, t=s, h=heads, tt=min(64, s))
    att = _attn_prompt(q, ka, va, b=b, s=s)
    y_prompt = _post(x_prompt.reshape(b * s, d), rw, g, att, p_prompt[0].reshape(b * s, -1), wts,
                     precise=False, tm=min(512, b * s), ne_step=4).reshape(b, s, d)
    prompt_k_win = ka.reshape(b, s, heads, HEAD_DIM)[:, s - keep:][None]
    prompt_v_win = va.reshape(b, s, heads, HEAD_DIM)[:, s - keep:][None]
    prompt_shift = plast[None]

    n_s = db * t
    pos_s = _rope_tables(np.tile(PAST_LEN + np.arange(t), db))
    sprev = jnp.repeat(padc(state_shift[0]), t, axis=0)
    r, k, v, w, a, g, q, ka, va, pfull = _inproj(
        x_sample.reshape(n_s, d), sprev, pos_s, wts, precise=True, seq_len=t, tm=n_s)
    lane_s = lambda z: _to_lane_layout(z, db, t, heads)
    rw, sample_wkv = _wkv(lane_s(r), lane_s(k), lane_s(v), lane_s(w), lane_s(a), state_wkv[0], wts,
                          b=db, t=t, h=heads, tt=t)
    wc = cache_k_win.shape[2]
    att = _attn_decode(q, ka, va, cache_k_win[0].reshape(db, wc, cw), cache_v_win[0].reshape(db, wc, cw),
                       b=db, t=t)
    y_sample = _post(x_sample.reshape(n_s, d), rw, g, att, p_sample[0].reshape(n_s, -1), wts,
                     precise=True, tm=n_s, ne_step=2).reshape(db, t, d)
    sample_k_rows = ka.reshape(db, t, heads, HEAD_DIM)[None]
    sample_v_rows = va.reshape(db, t, heads, HEAD_DIM)[None]
    sample_shift = _unpad_shift(pfull.reshape(db, t, cp)[:, t - 1], wts)[None]

    return (y_prompt, y_sample, prompt_k_win, prompt_v_win, prompt_wkv[None], prompt_shift,
            sample_k_rows, sample_v_rows, sample_wkv[None], sample_shift)
```

```python
import functools
import math

import jax
import jax.numpy as jnp
import numpy as np
from jax import lax
from jax.experimental import pallas as pl
from jax.experimental.pallas import tpu as pltpu

F32 = jnp.float32
BF16 = jnp.bfloat16

HEAD_DIM = 64
DECAY_LORA = 32
AAA_LORA = 32
GATE_LORA = 96
LORA_PAD = 256
DILATED = ((128, 1), (512, 4), (2048, 16))
WIN_MAX = 2048
ROPE_THETA = 10000.0
N_GROUPS = 4
EXPERTS_PER_GROUP = 8
N_EXPERTS = N_GROUPS * EXPERTS_PER_GROUP
PAST_LEN = 16384
RMS_EPS = 1e-6
LNX_EPS = 1e-5 * HEAD_DIM
NEG_INF = -1e30
LANES = 128
SUBLANES = 8
VMEM_LIMIT = 52 * 1024 * 1024
ATTN_UNROLL = 4


def _dot(a, b, precise):
    if precise:
        return jnp.dot(a, b, preferred_element_type=F32, precision=lax.Precision.HIGHEST)
    return jnp.dot(a.astype(BF16), b.astype(BF16), preferred_element_type=F32)


def _dot_nt(a, b, precise):
    dn = (((1,), (1,)), ((), ()))
    if precise:
        return lax.dot_general(a, b, dn, preferred_element_type=F32, precision=lax.Precision.HIGHEST)
    return lax.dot_general(a.astype(BF16), b.astype(BF16), dn, preferred_element_type=F32)


def _div_pow2(x, n):
    assert n & (n - 1) == 0
    return lax.shift_right_arithmetic(x, jnp.int32(n.bit_length() - 1))


def _mod_pow2(x, n):
    assert n & (n - 1) == 0
    return jnp.bitwise_and(x, n - 1)


def _sigmoid(x):
    return 1.0 / (1.0 + jnp.exp(-x))


def _rms(x, g):
    return x * lax.rsqrt(jnp.mean(x * x, axis=-1, keepdims=True) + RMS_EPS) * g


def _inproj_kernel(x_ref, nw_ref, w_ref, mu_ref, sprev_ref, cos_ref, sin_ref,
                   w0_ref, w2_ref, a0_ref, a2_ref, g2_ref,
                   r_ref, k_ref, v_ref, dec_ref, a_ref, g_ref, q_ref, ka_ref, va_ref, plast_ref,
                   carry_ref, *, precise, seq_len, tiles_per_seq, cw, cp):
    tm = x_ref.shape[0]
    u = _rms(x_ref[...], nw_ref[...])
    proj = _dot(u, w_ref[...], precise)
    pc = proj[:, :cp]
    row = lax.broadcasted_iota(jnp.int32, (tm, 1), 0)
    prev = pltpu.roll(pc, 1, axis=0)
    if tiles_per_seq is None:
        prev = jnp.where(_mod_pow2(row, seq_len) == 0, sprev_ref[...], prev)
        plast_ref[...] = pc
    else:
        j = pl.program_id(0) % tiles_per_seq
        first = jnp.where(j == 0, sprev_ref[0], carry_ref[...])
        prev = jnp.where(row == 0, first, prev)
        carry_ref[...] = pc[tm - 1:tm, :]
        plast_ref[0] = pc[tm - 1:tm, :]
    pm = pc + (prev - pc) * mu_ref[...]
    r_ref[...] = pm[:, 0:cw]
    k_ref[...] = pm[:, cw:2 * cw]
    v_ref[...] = pm[:, 2 * cw:3 * cw]
    lora = pm[:, 3 * cw:cp]
    zw = w0_ref[...] + _dot(jnp.tanh(lora), w2_ref[...], precise)
    nz = -zw
    softplus = jnp.maximum(nz, 0.0) + jnp.log(1.0 + jnp.exp(-jnp.abs(nz)))
    dec_ref[...] = jnp.exp(-jnp.exp(-softplus - 0.5))
    a_ref[...] = _sigmoid(a0_ref[...] + _dot(lora, a2_ref[...], precise))
    g_ref[...] = _dot(_sigmoid(lora), g2_ref[...], precise)

    cos = jnp.concatenate([cos_ref[...]] * (cw // LANES), axis=1)
    sin = jnp.concatenate([sin_ref[...]] * (cw // LANES), axis=1)
    lane = lax.broadcasted_iota(jnp.int32, (tm, cw), 1)
    lo_half = (lane % HEAD_DIM) < (HEAD_DIM // 2)

    def rope(t):
        partner = jnp.where(lo_half, pltpu.roll(t, cw - HEAD_DIM // 2, axis=1),
                            pltpu.roll(t, HEAD_DIM // 2, axis=1))
        return t * cos + partner * sin

    q_ref[...] = rope(proj[:, cp:cp + cw])
    ka_ref[...] = rope(proj[:, cp + cw:cp + 2 * cw])
    va_ref[...] = proj[:, cp + 2 * cw:cp + 3 * cw]


def _inproj(x2d, sprev, pos_tab, wts, *, precise, seq_len, tm):
    n, d = x2d.shape
    cw, cp = wts["cw"], wts["cp"]
    flat = tm % seq_len == 0 and tm >= seq_len
    nt = n // tm
    tiles_per_seq = None if flat else seq_len // tm
    cos_tab, sin_tab = pos_tab
    ntab = cos_tab.shape[0] // tm
    wdt = F32 if precise else BF16

    const = lambda shape: pl.BlockSpec(shape, lambda i: (0,) * len(shape))
    tok = lambda c: pl.BlockSpec((tm, c), lambda i: (i, 0))
    if flat:
        sprev_spec = pl.BlockSpec((tm, cp), lambda i: (i, 0))
        plast_shape = jax.ShapeDtypeStruct((n, cp), F32)
        plast_spec = pl.BlockSpec((tm, cp), lambda i: (i, 0))
    else:
        sprev_spec = pl.BlockSpec((1, 1, cp), lambda i: (i // tiles_per_seq, 0, 0))
        plast_shape = jax.ShapeDtypeStruct((n // seq_len, 1, cp), F32)
        plast_spec = pl.BlockSpec((1, 1, cp), lambda i: (i // tiles_per_seq, 0, 0))
    in_specs = [
        tok(d), const((1, d)), const((d, cp + 3 * cw)), const((1, cp)), sprev_spec,
        pl.BlockSpec((tm, LANES), lambda i: (i % ntab, 0)),
        pl.BlockSpec((tm, LANES), lambda i: (i % ntab, 0)),
        const((1, cw)), const((LORA_PAD, cw)), const((1, cw)), const((LORA_PAD, cw)),
        const((LORA_PAD, cw)),
    ]
    out_shape = [jax.ShapeDtypeStruct((n, cw), F32)] * 9 + [plast_shape]
    out_specs = [tok(cw)] * 9 + [plast_spec]
    kern = functools.partial(_inproj_kernel, precise=precise, seq_len=seq_len,
                             tiles_per_seq=tiles_per_seq, cw=cw, cp=cp)
    return pl.pallas_call(
        kern, grid=(nt,), in_specs=in_specs, out_specs=out_specs, out_shape=out_shape,
        scratch_shapes=[pltpu.VMEM((1, cp), F32)],
        compiler_params=pltpu.CompilerParams(dimension_semantics=("arbitrary",),
                                             vmem_limit_bytes=VMEM_LIMIT),
        name="inproj_precise" if precise else "inproj",
    )(x2d, wts["norm_mix"], wts["w_cat"].astype(wdt), wts["mu_cat"], sprev, cos_tab, sin_tab,
      wts["w0"], wts["w2p"].astype(wdt), wts["a0"], wts["a2p"].astype(wdt), wts["g2p"].astype(wdt))


def _inproj_cm_kernel(x_ref, nw_ref, wt_ref, wa_ref, mu_ref, sprev_ref, cos_ref, sin_ref,
                      w0_ref, w2_ref, a0_ref, a2_ref, g2_ref,
                      r_ref, k_ref, v_ref, dec_ref, a_ref, g_ref, q_ref, ka_ref, va_ref, plast_ref,
                      carry_ref, *, tiles_per_seq, cw, cp):
    tm = x_ref.shape[0]
    nh = cw // HEAD_DIM
    u = _rms(x_ref[...], nw_ref[...]).astype(BF16)
    pt = _dot_nt(wt_ref[...], u, False)
    j = pl.program_id(0) % tiles_per_seq
    plast_ref[0] = pt[:, tm - LANES:]
    lanes_of = lambda z: jnp.concatenate([z] * (tm // LANES), axis=1)
    cm = lambda z: z.reshape(HEAD_DIM, nh, tm)
    r_ref[...] = cm(pt[0:cw])
    k_ref[...] = cm(pt[cw:2 * cw])
    for i in range(v_ref.shape[1]):
        v_ref[:, i] = cm(pt[2 * cw:3 * cw])
    pl_ = pt[3 * cw:cp]
    before = jnp.where(j == 0, sprev_ref[0], carry_ref[...])
    lane = lax.broadcasted_iota(jnp.int32, (cp - 3 * cw, LANES), 1)
    prev = pltpu.roll(pl_, 1, axis=1)
    head_blk = jnp.where(lane == 0, pltpu.roll(before, 1, axis=1), prev[:, :LANES])
    prev = jnp.concatenate([head_blk, prev[:, LANES:]], axis=1) if tm > LANES else head_blk
    carry_ref[...] = pl_[:, tm - LANES:]
    lora = pl_ + (prev - pl_) * lanes_of(mu_ref[...])
    zw = lanes_of(w0_ref[...]) + _dot(w2_ref[...], jnp.tanh(lora), False)
    nz = -zw
    softplus = jnp.maximum(nz, 0.0) + jnp.log(1.0 + jnp.exp(-jnp.abs(nz)))
    dec_ref[...] = cm(jnp.exp(-jnp.exp(-softplus - 0.5)))
    a_ref[...] = cm(_sigmoid(lanes_of(a0_ref[...]) + _dot(a2_ref[...], lora, False)))
    g_ref[...] = _dot(g2_ref[...], _sigmoid(lora), False).T

    proj = _dot(u, wa_ref[...], False)
    cos = jnp.concatenate([cos_ref[...]] * (cw // LANES), axis=1)
    sin = jnp.concatenate([sin_ref[...]] * (cw // LANES), axis=1)
    lane_c = lax.broadcasted_iota(jnp.int32, (tm, cw), 1)
    lo_half = _mod_pow2(lane_c, HEAD_DIM) < (HEAD_DIM // 2)

    def rope(t):
        partner = jnp.where(lo_half, pltpu.roll(t, cw - HEAD_DIM // 2, axis=1),
                            pltpu.roll(t, HEAD_DIM // 2, axis=1))
        return t * cos + partner * sin

    q_ref[...] = rope(proj[:, 0:cw])
    ka_ref[...] = rope(proj[:, cw:2 * cw])
    va_ref[...] = proj[:, 2 * cw:3 * cw]


def _inproj_cm(x2d, pos_tab, wts, *, b, seq_len, tm):
    n, d = x2d.shape
    cw, cp = wts["cw"], wts["cp"]
    nh = cw // HEAD_DIM
    lp = cp - 3 * cw
    tiles_per_seq = seq_len // tm
    cos_tab, sin_tab = pos_tab
    const = lambda shape: pl.BlockSpec(shape, lambda i: (0,) * len(shape))
    tok = lambda c: pl.BlockSpec((tm, c), lambda i: (i, 0))
    chan = pl.BlockSpec((HEAD_DIM, nh, tm), lambda i: (0, i // tiles_per_seq, i % tiles_per_seq))
    ndup = max(1, LANES // (b * nh))
    chan2 = pl.BlockSpec((HEAD_DIM, ndup, nh, tm), lambda i: (0, 0, i // tiles_per_seq, i % tiles_per_seq))
    per_b = lambda rows: pl.BlockSpec((1, rows, LANES), lambda i: (i // tiles_per_seq, 0, 0))
    in_specs = [
        tok(d), const((1, d)), const((cp, d)), const((d, 3 * cw)), const((lp, LANES)), per_b(lp),
        pl.BlockSpec((tm, LANES), lambda i: (i % tiles_per_seq, 0)),
        pl.BlockSpec((tm, LANES), lambda i: (i % tiles_per_seq, 0)),
        const((cw, LANES)), const((cw, LORA_PAD)), const((cw, LANES)), const((cw, LORA_PAD)),
        const((cw, LORA_PAD)),
    ]
    chan_shape = jax.ShapeDtypeStruct((HEAD_DIM, b * nh, seq_len), F32)
    chan2_shape = jax.ShapeDtypeStruct((HEAD_DIM, ndup, b * nh, seq_len), F32)
    out_shape = [chan_shape, chan_shape, chan2_shape, chan_shape, chan_shape] + [
        jax.ShapeDtypeStruct((n, cw), F32)] * 4 + [jax.ShapeDtypeStruct((b, cp, LANES), F32)]
    out_specs = [chan, chan, chan2, chan, chan] + [tok(cw)] * 4 + [per_b(cp)]
    perm = lambda z: jnp.transpose(z.reshape(z.shape[0], -1, nh, HEAD_DIM), (0, 1, 3, 2)).reshape(z.shape)
    rp = wts["rp"]
    w = wts["w_cat"]
    wt = jnp.concatenate([perm(w[:, :3 * cw]), w[:, 3 * cw:cp]], axis=1).T.astype(BF16)
    col = lambda z: jnp.broadcast_to(z.reshape(-1, 1), (z.size, LANES))
    zero_shift = jnp.zeros((b, lp, LANES), F32)
    outs = pl.pallas_call(
        functools.partial(_inproj_cm_kernel, tiles_per_seq=tiles_per_seq, cw=cw, cp=cp),
        grid=(n // tm,), in_specs=in_specs, out_specs=out_specs, out_shape=out_shape,
        scratch_shapes=[pltpu.VMEM((lp, LANES), F32)],
        compiler_params=pltpu.CompilerParams(dimension_semantics=("arbitrary",),
                                             vmem_limit_bytes=VMEM_LIMIT),
        name="inproj",
    )(x2d, wts["norm_mix"], wt, w[:, cp:].astype(BF16), col(wts["mu_cat"][:, 3 * cw:]), zero_shift,
      cos_tab, sin_tab,
      col(perm(wts["w0"])), perm(wts["w2p"]).T.astype(BF16), col(perm(wts["a0"])),
      perm(wts["a2p"]).T.astype(BF16), wts["g2p"].T.astype(BF16))
    last = outs[9][:, :, LANES - 1]
    unperm = lambda z: jnp.transpose(z.reshape(z.shape[0], -1, HEAD_DIM, nh), (0, 1, 3, 2)).reshape(z.shape)
    plast = jnp.concatenate([unperm(last[:, :3 * cw]), last[:, 3 * cw:rp]], axis=1)
    return list(outs[:9]) + [plast]


def _lane_fold(x, kpar):
    s = LANES // 2
    while s >= LANES // kpar:
        x = x + pltpu.roll(x, s, axis=x.ndim - 1)
        s //= 2
    return x


def _wkv_kernel(*refs, kpar, shift):
    if shift:
        (r_ref, k_ref, w_ref, a_ref, v_ref, mur_ref, muk_ref, muv_ref, p0r_ref, p0k_ref, p0v_ref,
         kk_ref, ka_ref, rk_ref, lnw_ref, lnb_ref, s0_ref,
         y_ref, s_ref, aop_ref, wr_ref, bop_ref, km_ref, sc_ref, vs_ref, cr_ref, ck_ref, cv_ref) = refs
    else:
        (r_ref, k_ref, w_ref, a_ref, v_ref, kk_ref, ka_ref, rk_ref, lnw_ref, lnb_ref, s0_ref,
         y_ref, s_ref, aop_ref, wr_ref, bop_ref, km_ref, sc_ref, vs_ref) = refs
    tt, nr, _ = r_ref.shape
    nv = v_ref.shape[1]
    nvb = nv // SUBLANES

    @pl.when(pl.program_id(1) == 0)
    def _():
        s_ref[...] = s0_ref[...]
        if shift:
            cr_ref[...] = p0r_ref[...]
            ck_ref[...] = p0k_ref[...]
            cv_ref[...] = p0v_ref[...]

    def shifted(x_ref, c_ref, mu_ref):
        raw = x_ref[...]
        prev = jnp.concatenate([c_ref[...][None], raw[:tt - 1]], axis=0) if tt > 1 else c_ref[...][None]
        c_ref[...] = raw[tt - 1]
        return raw + (prev - raw) * mu_ref[...]

    if shift:
        r = shifted(r_ref, cr_ref, mur_ref)
        k = shifted(k_ref, ck_ref, muk_ref)
        vs_ref[...] = shifted(v_ref, cv_ref, muv_ref)
    else:
        r = r_ref[...]
        k = k_ref[...]
        vs_ref[...] = v_ref[...]
    a = a_ref[...]
    kkr = k * kk_ref[...]
    ss = _lane_fold(jnp.sum(kkr * kkr, axis=1, keepdims=True), kpar)
    kkn = kkr / jnp.maximum(jnp.sqrt(ss), 1e-12)
    bop = kkn * a
    km = k * (1.0 + (a - 1.0) * ka_ref[...])
    aop_ref[...] = -kkn
    bop_ref[...] = bop
    km_ref[...] = km
    wr_ref[...] = w_ref[...] * r
    br = _lane_fold(jnp.sum(bop * r, axis=1, keepdims=True), kpar)
    kr = _lane_fold(jnp.sum(km * r, axis=1, keepdims=True), kpar)
    bonus = _lane_fold(jnp.sum(r * km * rk_ref[...], axis=1, keepdims=True), kpar)
    sc_ref[:, 0:1, :] = br
    sc_ref[:, 1:2, :] = kr

    def step(t, carry):
        vv = [vs_ref[t, vb * SUBLANES:(vb + 1) * SUBLANES, :] for vb in range(nvb)]
        acc_sa = [None] * nvb
        acc_y = [None] * nvb
        for kr_i in range(nr):
            a_row = aop_ref[t, kr_i:kr_i + 1, :]
            wr_row = wr_ref[t, kr_i:kr_i + 1, :]
            for vb in range(nvb):
                blk = s_ref[kr_i, vb * SUBLANES:(vb + 1) * SUBLANES, :]
                pa = blk * a_row
                py = blk * wr_row
                acc_sa[vb] = pa if acc_sa[vb] is None else acc_sa[vb] + pa
                acc_y[vb] = py if acc_y[vb] is None else acc_y[vb] + py
        sa = [_lane_fold(x, kpar) for x in acc_sa]
        yp = [_lane_fold(x, kpar) for x in acc_y]
        for kr_i in range(nr):
            w_row = w_ref[t, kr_i:kr_i + 1, :]
            b_row = bop_ref[t, kr_i:kr_i + 1, :]
            k_row = km_ref[t, kr_i:kr_i + 1, :]
            for vb in range(nvb):
                sl = slice(vb * SUBLANES, (vb + 1) * SUBLANES)
                s_ref[kr_i, sl, :] = s_ref[kr_i, sl, :] * w_row + sa[vb] * b_row + vv[vb] * k_row
        br_row = sc_ref[t, 0:1, :]
        kr_row = sc_ref[t, 1:2, :]
        for vb in range(nvb):
            y_ref[t, vb * SUBLANES:(vb + 1) * SUBLANES, :] = yp[vb] + sa[vb] * br_row + vv[vb] * kr_row
        return carry

    lax.fori_loop(0, tt, step, 0)

    y = y_ref[...]
    mean = jnp.mean(y, axis=1, keepdims=True)
    yc = y - mean
    var = jnp.mean(yc * yc, axis=1, keepdims=True)
    y_ref[...] = yc * lax.rsqrt(var + LNX_EPS) * lnw_ref[...] + lnb_ref[...] + bonus * vs_ref[...]


def _to_lane_layout(x, b, t, h):
    return jnp.transpose(x.reshape(b, t, h, HEAD_DIM), (1, 3, 0, 2)).reshape(t, HEAD_DIM, b * h)


def _wkv(r, k, v, w, a, state, prm, *, b, t, h, tt, shift=None):
    p = b * h
    if p >= LANES:
        assert p % LANES == 0
        g, kpar = p // LANES, 1
    else:
        assert LANES % p == 0
        g, kpar = 1, LANES // p
    nr = HEAD_DIM // kpar

    def krows(x):
        if kpar > 1:
            return x.reshape(1, x.shape[0], nr, LANES)
        return jnp.transpose(x.reshape(x.shape[0], HEAD_DIM, g, LANES), (2, 0, 1, 3))

    def vrows(x):
        if kpar > 1:
            return x[None] if x.shape[-1] == LANES else jnp.concatenate([x] * kpar, axis=-1)[None]
        return jnp.transpose(x.reshape(x.shape[0], HEAD_DIM, g, LANES), (2, 0, 1, 3))

    rl, kl, wl, al = (krows(z) for z in (r, k, w, a))
    vl = vrows(v)

    def prm_k(z):
        x = jnp.broadcast_to(z.T[:, None, :], (HEAD_DIM, b, h)).reshape(1, HEAD_DIM, p)
        return krows(x)[:, 0]

    def prm_v(z):
        x = jnp.broadcast_to(z.T[:, None, :], (HEAD_DIM, b, h)).reshape(1, HEAD_DIM, p)
        return vrows(x)[:, 0]

    st = jnp.transpose(state.reshape(p, HEAD_DIM, HEAD_DIM), (2, 1, 0))
    if kpar > 1:
        st = jnp.transpose(st.reshape(nr, kpar, HEAD_DIM, p), (0, 2, 1, 3)).reshape(1, nr, HEAD_DIM, LANES)
    else:
        st = jnp.transpose(st.reshape(HEAD_DIM, HEAD_DIM, g, LANES), (2, 0, 1, 3))

    nt = t // tt
    tile_k = pl.BlockSpec((None, tt, nr, LANES), lambda gi, ti: (gi, ti, 0, 0))
    tile_v = pl.BlockSpec((None, tt, HEAD_DIM, LANES), lambda gi, ti: (gi, ti, 0, 0))
    par_k = pl.BlockSpec((None, nr, LANES), lambda gi, ti: (gi, 0, 0))
    par_v = pl.BlockSpec((None, HEAD_DIM, LANES), lambda gi, ti: (gi, 0, 0))
    st_spec = pl.BlockSpec((None, nr, HEAD_DIM, LANES), lambda gi, ti: (gi, 0, 0, 0))
    shift_args, shift_specs, shift_scratch = [], [], []
    if shift is not None:
        mu_r, mu_k, mu_v, p0_r, p0_k, p0_v = shift
        lane0 = lambda z: _to_lane_layout(z, b, 1, h)
        shift_args = [prm_k(mu_r), prm_k(mu_k), prm_v(mu_v),
                      krows(lane0(p0_r))[:, 0], krows(lane0(p0_k))[:, 0], vrows(lane0(p0_v))[:, 0]]
        shift_specs = [par_k, par_k, par_v, par_k, par_k, par_v]
        shift_scratch = [pltpu.VMEM((nr, LANES), F32)] * 2 + [pltpu.VMEM((HEAD_DIM, LANES), F32)]
    y, s_fin = pl.pallas_call(
        functools.partial(_wkv_kernel, kpar=kpar, shift=shift is not None),
        grid=(g, nt),
        in_specs=[tile_k, tile_k, tile_k, tile_k, tile_v] + shift_specs
                 + [par_k, par_k, par_k, par_v, par_v, st_spec],
        out_specs=[tile_v, st_spec],
        out_shape=[jax.ShapeDtypeStruct((g, t, HEAD_DIM, LANES), F32),
                   jax.ShapeDtypeStruct((g, nr, HEAD_DIM, LANES), F32)],
        scratch_shapes=[pltpu.VMEM((tt, nr, LANES), F32)] * 4 + [pltpu.VMEM((tt, SUBLANES, LANES), F32),
                                                                 pltpu.VMEM((tt, HEAD_DIM, LANES), F32)]
                       + shift_scratch,
        compiler_params=pltpu.CompilerParams(dimension_semantics=("arbitrary", "arbitrary"),
                                             vmem_limit_bytes=VMEM_LIMIT),
        name="wkv",
    )(rl, kl, wl, al, vl, *shift_args, prm_k(prm["k_k"]), prm_k(prm["k_a"]), prm_k(prm["r_k"]),
      prm_v(prm["lnx_w"]), prm_v(prm["lnx_b"]), st)

    if kpar > 1:
        yt = y[0, :, :, :p]
        sf = jnp.transpose(s_fin[0].reshape(nr, HEAD_DIM, kpar, p), (3, 1, 0, 2))
    else:
        yt = jnp.transpose(y, (1, 2, 0, 3)).reshape(t, HEAD_DIM, p)
        sf = jnp.transpose(s_fin, (0, 3, 2, 1)).reshape(p, HEAD_DIM, HEAD_DIM)
    y_tok = jnp.transpose(yt.reshape(t, HEAD_DIM, b, h), (2, 0, 3, 1)).reshape(b * t, h * HEAD_DIM)
    return y_tok, sf.reshape(b, h, HEAD_DIM, HEAD_DIM)


def _attn_prompt_kernel(q_ref, k_ref, v_ref, o_ref, ob_ref, lb_ref, qs_ref, vs_ref, kt_ref,
                        bias2_ref, bias1_ref, *, unroll):
    s_len = q_ref.shape[0]
    blk = 128
    nchunk = s_len // blk
    lane = lax.broadcasted_iota(jnp.int32, (blk, LANES), 1)
    head0 = lane < HEAD_DIM
    head1 = jnp.logical_not(head0)
    qi = _mod_pow2(lax.broadcasted_iota(jnp.int32, (2 * blk, 2 * blk), 0), blk)
    kj = lax.broadcasted_iota(jnp.int32, (2 * blk, 2 * blk), 1)
    bias2_ref[...] = jnp.where(kj < blk, jnp.where(kj >= qi, 0.0, NEG_INF),
                               jnp.where(kj - blk <= qi, 0.0, NEG_INF))
    qi1 = _mod_pow2(lax.broadcasted_iota(jnp.int32, (2 * blk, blk), 0), blk)
    kj1 = lax.broadcasted_iota(jnp.int32, (2 * blk, blk), 1)
    bias1_ref[...] = jnp.where(kj1 <= qi1, 0.0, NEG_INF)
    scale = 1.0 / math.sqrt(HEAD_DIM)

    def finish(bi, rows, s, v0, v1):
        s0, s1 = s[:blk], s[blk:]
        m0 = jnp.max(s0, axis=1, keepdims=True)
        m1 = jnp.max(s1, axis=1, keepdims=True)
        a0 = jnp.dot(jnp.exp(s0 - m0).astype(BF16), v0, preferred_element_type=F32)
        a1 = jnp.dot(jnp.exp(s1 - m1).astype(BF16), v1, preferred_element_type=F32)
        num = jnp.where(head0, a0, a1)
        den = pltpu.roll(jnp.where(head0, a1, a0), HEAD_DIM, axis=1)
        ob_ref[bi, rows, :] = num / den
        lb_ref[bi, rows, :] = jnp.where(head0, m0, m1) + jnp.log(den)

    for bi, (window, dil) in enumerate(DILATED):
        assert window // dil == blk
        unit = blk * dil
        nblk = s_len // unit

        def pos_rows(j, dil=dil, unit=unit, nblk=nblk):
            if dil == 1:
                return pl.ds(pl.multiple_of(j * blk, blk), blk)
            return pl.ds((j % nblk) * unit + j // nblk, blk, stride=dil)

        def stage(j, carry, pos_rows=pos_rows):
            rows = pos_rows(j)
            dst = pl.ds(pl.multiple_of(j * blk, blk), blk)
            q = q_ref[rows, :] * scale
            v = v_ref[rows, :]
            qs_ref[0, dst, :] = jnp.where(head0, q, 0.0).astype(BF16)
            qs_ref[1, dst, :] = jnp.where(head1, q, 0.0).astype(BF16)
            vs_ref[0, dst, :] = jnp.where(head0, v, 1.0).astype(BF16)
            vs_ref[1, dst, :] = jnp.where(head1, v, 1.0).astype(BF16)
            kt_ref[:, dst] = k_ref[rows, :].T.astype(BF16)
            return carry

        def first(c, carry, bi=bi, nblk=nblk, pos_rows=pos_rows):
            j = c * nblk
            src = pl.ds(pl.multiple_of(j * blk, blk), blk)
            qs = jnp.concatenate([qs_ref[0, src, :], qs_ref[1, src, :]], axis=0)
            s = jnp.dot(qs, kt_ref[:, src], preferred_element_type=F32) + bias1_ref[...]
            finish(bi, pos_rows(j), s, vs_ref[0, src, :], vs_ref[1, src, :])
            return carry

        def later(idx, carry, bi=bi, nblk=nblk, pos_rows=pos_rows):
            j = (idx // (nblk - 1)) * nblk + 1 + idx % (nblk - 1)
            src = pl.ds(pl.multiple_of(j * blk, blk), blk)
            keys = pl.ds(pl.multiple_of((j - 1) * blk, blk), 2 * blk)
            qs = jnp.concatenate([qs_ref[0, src, :], qs_ref[1, src, :]], axis=0)
            s = jnp.dot(qs, kt_ref[:, keys], preferred_element_type=F32) + bias2_ref[...]
            finish(bi, pos_rows(j), s, vs_ref[0, keys, :], vs_ref[1, keys, :])
            return carry

        lax.fori_loop(0, nchunk, stage, 0, unroll=2)
        lax.fori_loop(0, dil, first, 0, unroll=min(dil, unroll))
        if nblk > 1:
            lax.fori_loop(0, dil * (nblk - 1), later, 0, unroll=unroll)

    def merge(ci, carry):
        sl = pl.ds(pl.multiple_of(ci * blk, blk), blk)
        l0, l1, l2 = lb_ref[0, sl, :], lb_ref[1, sl, :], lb_ref[2, sl, :]
        m = jnp.maximum(jnp.maximum(l0, l1), l2)
        w0, w1, w2 = jnp.exp(l0 - m), jnp.exp(l1 - m), jnp.exp(l2 - m)
        tot = w0 + w1 + w2
        o_ref[sl, :] = (w0 * ob_ref[0, sl, :] + w1 * ob_ref[1, sl, :] + w2 * ob_ref[2, sl, :]) / tot
        return carry

    lax.fori_loop(0, s_len // blk, merge, 0)


def _attn_prompt(q, k, v, *, b, s):
    cw = q.shape[1]
    hp = cw // LANES
    assert s % (128 * DILATED[-1][1]) == 0
    spec = pl.BlockSpec((None, s, LANES), lambda bi, hi: (bi, 0, hi))
    out = pl.pallas_call(
        functools.partial(_attn_prompt_kernel, unroll=ATTN_UNROLL), grid=(b, hp),
        in_specs=[spec, spec, spec], out_specs=spec,
        out_shape=jax.ShapeDtypeStruct((b, s, cw), F32),
        scratch_shapes=[pltpu.VMEM((3, s, LANES), F32), pltpu.VMEM((3, s, LANES), F32),
                        pltpu.VMEM((2, s, LANES), BF16), pltpu.VMEM((2, s, LANES), BF16),
                        pltpu.VMEM((LANES, s), BF16),
                        pltpu.VMEM((256, 256), F32), pltpu.VMEM((256, 128), F32)],
        compiler_params=pltpu.CompilerParams(dimension_semantics=("arbitrary", "arbitrary"),
                                             vmem_limit_bytes=VMEM_LIMIT),
        name="attn_prompt",
    )(q.reshape(b, s, cw), k.reshape(b, s, cw), v.reshape(b, s, cw))
    return out.reshape(b * s, cw)


def _attn_decode_kernel(q_ref, kn_ref, vn_ref, kc_ref, vc_ref, o_ref, *, t_new):
    wc, cw = kc_ref.shape
    nh = cw // HEAD_DIM
    npad = kn_ref.shape[0]
    rows = nh * t_new
    q = q_ref[...]
    qt = jnp.concatenate([q] * nh, axis=0)
    rh = _div_pow2(lax.broadcasted_iota(jnp.int32, (rows, cw), 0), t_new)
    lh = _div_pow2(lax.broadcasted_iota(jnp.int32, (rows, cw), 1), HEAD_DIM)
    own_head = rh == lh
    qbd = jnp.where(own_head, qt, 0.0)
    scale = 1.0 / math.sqrt(HEAD_DIM)
    s_c = _dot_nt(qbd, kc_ref[...], True) * scale
    s_n = _dot_nt(qbd, kn_ref[...], True) * scale
    t_c = _mod_pow2(lax.broadcasted_iota(jnp.int32, (rows, wc), 0), t_new)
    dist_c = wc + t_c - lax.broadcasted_iota(jnp.int32, (rows, wc), 1)
    t_n = _mod_pow2(lax.broadcasted_iota(jnp.int32, (rows, npad), 0), t_new)
    j_n = lax.broadcasted_iota(jnp.int32, (rows, npad), 1)
    dist_n = t_n - j_n
    ecs, ens, ls, lses = [], [], [], []
    for window, dil in DILATED:
        ok_c = jnp.logical_and(_mod_pow2(dist_c, dil) == 0, dist_c <= window)
        ok_n = jnp.logical_and(jnp.logical_and(dist_n >= 0, j_n < t_new),
                               jnp.logical_and(_mod_pow2(dist_n, dil) == 0, dist_n <= window))
        sc = jnp.where(ok_c, s_c, NEG_INF)
        sn = jnp.where(ok_n, s_n, NEG_INF)
        m = jnp.maximum(jnp.max(sc, axis=1, keepdims=True), jnp.max(sn, axis=1, keepdims=True))
        ec = jnp.exp(sc - m)
        en = jnp.exp(sn - m)
        l = jnp.sum(ec, axis=1, keepdims=True) + jnp.sum(en, axis=1, keepdims=True)
        ecs.append(ec)
        ens.append(en)
        ls.append(l)
        lses.append(m + jnp.log(l))
    o_all = (_dot(jnp.concatenate(ecs, axis=0), vc_ref[...], True)
             + _dot(jnp.concatenate(ens, axis=0), vn_ref[...], True))
    outs = [o_all[i * rows:(i + 1) * rows] / ls[i] for i in range(len(DILATED))]
    m = jnp.maximum(jnp.maximum(lses[0], lses[1]), lses[2])
    ws = [jnp.exp(x - m) for x in lses]
    full = (ws[0] * outs[0] + ws[1] * outs[1] + ws[2] * outs[2]) / (ws[0] + ws[1] + ws[2])
    full = jnp.where(own_head, full, 0.0).reshape(nh, t_new, cw)
    o_ref[...] = jnp.sum(full, axis=0)


def _attn_decode(q, k, v, cache_k, cache_v, *, b, t):
    cw = q.shape[1]
    wc = cache_k.shape[1]
    npad = LANES
    pad = lambda z: jnp.pad(z.reshape(b, t, cw), ((0, 0), (0, npad - t), (0, 0)))
    new_spec = pl.BlockSpec((None, npad, cw), lambda bi: (bi, 0, 0))
    cache_spec = pl.BlockSpec((None, wc, cw), lambda bi: (bi, 0, 0))
    tok_spec = pl.BlockSpec((None, t, cw), lambda bi: (bi, 0, 0))
    out = pl.pallas_call(
        functools.partial(_attn_decode_kernel, t_new=t), grid=(b,),
        in_specs=[tok_spec, new_spec, new_spec, cache_spec, cache_spec], out_specs=tok_spec,
        out_shape=jax.ShapeDtypeStruct((b, t, cw), F32),
        compiler_params=pltpu.CompilerParams(dimension_semantics=("arbitrary",),
                                             vmem_limit_bytes=VMEM_LIMIT),
        name="attn_decode",
    )(q.reshape(b, t, cw), pad(k), pad(v), cache_k, cache_v)
    return out.reshape(b * t, cw)


def _route(logits):
    lane_i = lax.broadcasted_iota(jnp.int32, logits.shape, 1)
    lane = lane_i.astype(F32)
    is_g = lane_i < N_GROUPS
    lg = jnp.where(is_g, logits, NEG_INF)
    mg = jnp.max(lg, axis=1, keepdims=True)
    pg_sel = 1.0 / jnp.sum(jnp.where(is_g, jnp.exp(lg - mg), 0.0), axis=1, keepdims=True)
    gsel = jnp.min(jnp.where(lg == mg, lane, float(LANES)), axis=1, keepdims=True)
    e_lane = lane_i - N_GROUPS
    e_group = _div_pow2(e_lane, EXPERTS_PER_GROUP).astype(F32)
    in_sel = jnp.logical_and(jnp.logical_and(e_lane >= 0, e_lane < N_EXPERTS), e_group == gsel)
    le = jnp.where(in_sel, logits, NEG_INF)
    me = jnp.max(le, axis=1, keepdims=True)
    ee = jnp.where(in_sel, jnp.exp(le - me), 0.0)
    pe = jnp.where(in_sel, ee / jnp.sum(ee, axis=1, keepdims=True), -1.0)
    p1 = jnp.max(pe, axis=1, keepdims=True)
    i1 = jnp.min(jnp.where(pe == p1, lane, float(LANES)), axis=1, keepdims=True)
    pe2 = jnp.where(lane == i1, -1.0, pe)
    p2 = jnp.max(pe2, axis=1, keepdims=True)
    i2 = jnp.min(jnp.where(pe2 == p2, lane, float(LANES)), axis=1, keepdims=True)
    tot = p1 + p2
    return jnp.where(lane == i1, pg_sel * p1 / tot, 0.0) + jnp.where(lane == i2, pg_sel * p2 / tot, 0.0)


def _post_kernel(x_ref, rw_ref, g_ref, att_ref, p_ref, an_ref, wo_ref, nf_ref, wr_ref, br_ref,
                 wg_ref, wu_ref, wd_ref, np_ref, wpg_ref, wpp_ref, nfin_ref,
                 y_ref, h1_ref, u2_ref, gate_ref, acc_ref, *, precise):
    e = pl.program_id(1)
    cw = rw_ref.shape[1]
    ne_step, ff, _ = wd_ref.shape

    @pl.when(e == 0)
    def _():
        mix_r = rw_ref[...] * g_ref[...]
        mix_a = _rms(att_ref[...], an_ref[...])
        h1 = x_ref[...] + _dot(mix_r, wo_ref[0:cw, :], precise) + _dot(mix_a, wo_ref[cw:2 * cw, :], precise)
        h1_ref[...] = h1
        u2 = _rms(h1, nf_ref[...])
        u2_ref[...] = u2.astype(u2_ref.dtype)
        gate_ref[...] = _route(_dot(u2, wr_ref[...], True) + br_ref[...])
        acc_ref[...] = jnp.zeros_like(acc_ref)

    lane = lax.broadcasted_iota(jnp.int32, gate_ref.shape, 1)
    u2 = u2_ref[...]
    gate = gate_ref[...]
    hidden = []
    for i in range(ne_step):
        ge = jnp.sum(jnp.where(lane == e * ne_step + i + N_GROUPS, gate, 0.0), axis=1, keepdims=True)
        gp = _dot(u2, wg_ref[i], precise)
        up = _dot(u2, wu_ref[i], precise)
        hidden.append((gp * _sigmoid(gp) * up * ge).astype(u2.dtype))
    acc_ref[...] += _dot(jnp.concatenate(hidden, axis=1), wd_ref[...].reshape(ne_step * ff, -1), precise)

    @pl.when(e == pl.num_programs(1) - 1)
    def _():
        h2 = h1_ref[...] + acc_ref[...]
        u3 = _rms(h2, np_ref[...])
        h3 = h2 + _sigmoid(_dot(u3, wpg_ref[...], precise)) * _dot(p_ref[...], wpp_ref[...], precise)
        y_ref[...] = _rms(h3, nfin_ref[...])


def _post(x2d, rw, g, att, p2d, wts, *, precise, tm, ne_step):
    n, d = x2d.shape
    cw = rw.shape[1]
    pd = p2d.shape[1]
    ne, _, ff = wts["w_g"].shape
    assert ne % ne_step == 0
    wdt = F32 if precise else BF16
    tok = lambda c: pl.BlockSpec((tm, c), lambda i, e: (i, 0))
    const = lambda shape: pl.BlockSpec(shape, lambda i, e: (0,) * len(shape))
    in_specs = [
        tok(d), tok(cw), tok(cw), tok(cw), tok(pd),
        const((1, cw)), const((2 * cw, d)), const((1, d)), const((d, LANES)), const((1, LANES)),
        pl.BlockSpec((ne_step, d, ff), lambda i, e: (e, 0, 0)),
        pl.BlockSpec((ne_step, d, ff), lambda i, e: (e, 0, 0)),
        pl.BlockSpec((ne_step, ff, d), lambda i, e: (e, 0, 0)),
        const((1, d)), const((d, d)), const((pd, d)), const((1, d)),
    ]
    return pl.pallas_call(
        functools.partial(_post_kernel, precise=precise),
        grid=(n // tm, ne // ne_step), in_specs=in_specs, out_specs=tok(d),
        out_shape=jax.ShapeDtypeStruct((n, d), F32),
        scratch_shapes=[pltpu.VMEM((tm, d), F32), pltpu.VMEM((tm, d), wdt),
                        pltpu.VMEM((tm, LANES), F32), pltpu.VMEM((tm, d), F32)],
        compiler_params=pltpu.CompilerParams(dimension_semantics=("arbitrary", "arbitrary"),
                                             vmem_limit_bytes=VMEM_LIMIT),
        name="post_precise" if precise else "post",
    )(x2d, rw, g, att, p2d, wts["attn_out_norm"], wts["w_out"].astype(wdt), wts["norm_ffn"],
      wts["w_router"], wts["b_router"], wts["w_g"].astype(wdt), wts["w_u"].astype(wdt),
      wts["w_d"].astype(wdt),
      wts["norm_ple"], wts["ple_gate_w"].astype(wdt), wts["ple_proj"].astype(wdt), wts["norm_final"])


def _rope_tables(pos):
    half = HEAD_DIM // 2
    inv = np.power(ROPE_THETA, -np.arange(half, dtype=np.float64) * 2.0 / HEAD_DIM)
    ang = np.asarray(pos, np.float64)[:, None] * inv[None, :]
    cos, sin = np.cos(ang), np.sin(ang)
    cos_h = np.concatenate([cos, cos], axis=1)
    sin_h = np.concatenate([-sin, sin], axis=1)
    reps = LANES // HEAD_DIM
    return (jnp.asarray(np.concatenate([cos_h] * reps, axis=1), F32),
            jnp.asarray(np.concatenate([sin_h] * reps, axis=1), F32))


def _layer_weights(i, norm_mix, w_in, mu_shift, decay_w0, decay_w2, iclr_a0, iclr_a2, gate_g2,
                   k_k, k_a, r_k, lnx_w, lnx_b, attn_out_norm, w_out, norm_ffn,
                   router_group_w, router_group_b, router_expert_w, router_expert_b,
                   expert_w_gate, expert_w_up, expert_w_down, norm_ple, ple_gate_w, ple_proj, norm_final):
    d = w_in.shape[1]
    cw = decay_w0.shape[1]
    n_lora = DECAY_LORA + AAA_LORA + GATE_LORA
    rp = 3 * cw + n_lora
    cp = 3 * cw + LORA_PAD
    padc = lambda z: jnp.pad(z, ((0, 0), (0, LORA_PAD - n_lora)))
    w = w_in[i]
    w_cat = jnp.concatenate([padc(w[:, :rp]), w[:, rp:]], axis=1)
    padr = lambda z, lo: jnp.pad(z, ((lo, LORA_PAD - lo - z.shape[0]), (0, 0)))
    n_route = N_GROUPS + N_EXPERTS
    heads = cw // HEAD_DIM
    return dict(
        cw=cw, cp=cp, rp=rp,
        norm_mix=norm_mix[i][None], w_cat=w_cat, mu_cat=padc(mu_shift[i][None]),
        w0=decay_w0[i][None], w2p=padr(decay_w2[i], 0), a0=iclr_a0[i][None],
        a2p=padr(iclr_a2[i], DECAY_LORA), g2p=padr(gate_g2[i], DECAY_LORA + AAA_LORA),
        k_k=k_k[i].reshape(heads, HEAD_DIM), k_a=k_a[i].reshape(heads, HEAD_DIM), r_k=r_k[i],
        lnx_w=lnx_w[i].reshape(heads, HEAD_DIM), lnx_b=lnx_b[i].reshape(heads, HEAD_DIM),
        attn_out_norm=attn_out_norm[i][None], w_out=w_out[i], norm_ffn=norm_ffn[i][None],
        w_router=jnp.pad(jnp.concatenate([router_group_w[i], router_expert_w[i]], axis=1),
                         ((0, 0), (0, LANES - n_route))),
        b_router=jnp.pad(jnp.concatenate([router_group_b[i], router_expert_b[i]])[None],
                         ((0, 0), (0, LANES - n_route))),
        w_g=expert_w_gate[i], w_u=expert_w_up[i], w_d=expert_w_down[i],
        norm_ple=norm_ple[i][None], ple_gate_w=ple_gate_w[i], ple_proj=ple_proj[i],
        norm_final=norm_final[None],
    )


def _unpad_shift(pl_rows, wts):
    return pl_rows[:, :wts["rp"]]


def kernel(x_prompt, x_sample, cache_k_win, cache_v_win, state_wkv, state_shift, p_prompt, p_sample, norm_mix, w_in, mu_shift, decay_w0, decay_w2, iclr_a0, iclr_a2, gate_g2, k_k, k_a, r_k, lnx_w, lnx_b, attn_out_norm, w_out, norm_ffn, router_group_w, router_group_b, router_expert_w, router_expert_b, expert_w_gate, expert_w_up, expert_w_down, norm_ple, ple_gate_w, ple_proj, norm_final):
    b, s, d = x_prompt.shape
    db, t, _ = x_sample.shape
    depth = w_in.shape[0]
    assert depth == 1, "a deeper stack would chain the layer below over h"
    wts = _layer_weights(0, norm_mix, w_in, mu_shift, decay_w0, decay_w2, iclr_a0, iclr_a2, gate_g2,
                         k_k, k_a, r_k, lnx_w, lnx_b, attn_out_norm, w_out, norm_ffn,
                         router_group_w, router_group_b, router_expert_w, router_expert_b,
                         expert_w_gate, expert_w_up, expert_w_down, norm_ple, ple_gate_w, ple_proj,
                         norm_final)
    cw, cp, rp = wts["cw"], wts["cp"], wts["rp"]
    heads = cw // HEAD_DIM
    keep = min(WIN_MAX, s)
    padc = lambda z: jnp.pad(z, ((0, 0), (0, cp - rp)))

    tm_p = min(512, s)
    pos_p = _rope_tables(np.arange(s))
    r, k, v, w, a, g, q, ka, va, plast = _inproj_cm(
        x_prompt.reshape(b * s, d), pos_p, wts, b=b, seq_len=s, tm=tm_p)
    zero_wkv = jnp.zeros((b, heads, HEAD_DIM, HEAD_DIM), F32)
    lane_p = lambda z: z.reshape(-1, s).T.reshape(s, HEAD_DIM, -1)
    mu = wts["mu_cat"][0, :3 * cw].reshape(3, heads, HEAD_DIM)
    zero_p = jnp.zeros((b, cw), F32)
    rw, prompt_wkv = _wkv(lane_p(r), lane_p(k), lane_p(v), lane_p(w), lane_p(a), zero_wkv, wts,
                          b=b, t=s, h=heads, tt=min(64, s),
                          shift=(mu[0], mu[1], mu[2], zero_p, zero_p, zero_p))
    att = _attn_prompt(q, ka, va, b=b, s=s)
    y_prompt = _post(x_prompt.reshape(b * s, d), rw, g, att, p_prompt[0].reshape(b * s, -1), wts,
                     precise=False, tm=min(512, b * s), ne_step=4).reshape(b, s, d)
    prompt_k_win = ka.reshape(b, s, heads, HEAD_DIM)[:, s - keep:][None]
    prompt_v_win = va.reshape(b, s, heads, HEAD_DIM)[:, s - keep:][None]
    prompt_shift = plast[None]

    n_s = db * t
    pos_s = _rope_tables(np.tile(PAST_LEN + np.arange(t), db))
    sprev = jnp.repeat(padc(state_shift[0]), t, axis=0)
    r, k, v, w, a, g, q, ka, va, pfull = _inproj(
        x_sample.reshape(n_s, d), sprev, pos_s, wts, precise=True, seq_len=t, tm=n_s)
    lane_s = lambda z: _to_lane_layout(z, db, t, heads)
    rw, sample_wkv = _wkv(lane_s(r), lane_s(k), lane_s(v), lane_s(w), lane_s(a), state_wkv[0], wts,
                          b=db, t=t, h=heads, tt=t)
    wc = cache_k_win.shape[2]
    att = _attn_decode(q, ka, va, cache_k_win[0].reshape(db, wc, cw), cache_v_win[0].reshape(db, wc, cw),
                       b=db, t=t)
    y_sample = _post(x_sample.reshape(n_s, d), rw, g, att, p_sample[0].reshape(n_s, -1), wts,
                     precise=True, tm=n_s, ne_step=2).reshape(db, t, d)
    sample_k_rows = ka.reshape(db, t, heads, HEAD_DIM)[None]
    sample_v_rows = va.reshape(db, t, heads, HEAD_DIM)[None]
    sample_shift = _unpad_shift(pfull.reshape(db, t, cp)[:, t - 1], wts)[None]

    return (y_prompt, y_sample, prompt_k_win, prompt_v_win, prompt_wkv[None], prompt_shift,
            sample_k_rows, sample_v_rows, sample_wkv[None], sample_shift)
```

```python
import functools
import math

import jax
import jax.numpy as jnp
import numpy as np
from jax import lax
from jax.experimental import pallas as pl
from jax.experimental.pallas import tpu as pltpu

F32 = jnp.float32
BF16 = jnp.bfloat16

HEAD_DIM = 64
DECAY_LORA = 32
AAA_LORA = 32
GATE_LORA = 96
LORA_PAD = 256
DILATED = ((128, 1), (512, 4), (2048, 16))
WIN_MAX = 2048
ROPE_THETA = 10000.0
N_GROUPS = 4
EXPERTS_PER_GROUP = 8
N_EXPERTS = N_GROUPS * EXPERTS_PER_GROUP
PAST_LEN = 16384
RMS_EPS = 1e-6
LNX_EPS = 1e-5 * HEAD_DIM
NEG_INF = -1e30
LANES = 128
SUBLANES = 8
VMEM_LIMIT = 52 * 1024 * 1024
ATTN_UNROLL = 4


def _dot(a, b, precise):
    if precise:
        return _dot3(a, b)
    return jnp.dot(a.astype(BF16), b.astype(BF16), preferred_element_type=F32)


def _dot_nt(a, b, precise):
    if precise:
        return _dot3(a, b, nt=True)
    dn = (((1,), (1,)), ((), ()))
    return lax.dot_general(a.astype(BF16), b.astype(BF16), dn, preferred_element_type=F32)


def _split_bf16(a):
    hi = a.astype(BF16)
    return hi, (a - hi.astype(F32)).astype(BF16)


def _dot3(a, b, nt=False):
    dn = (((1,), (1 if nt else 0,)), ((), ()))
    mm = lambda x, y: lax.dot_general(x, y, dn, preferred_element_type=F32)
    a_hi, a_lo = _split_bf16(a)
    b_hi, b_lo = _split_bf16(b)
    return mm(a_hi, b_hi) + (mm(a_hi, b_lo) + mm(a_lo, b_hi))


def _div_pow2(x, n):
    assert n & (n - 1) == 0
    return lax.shift_right_arithmetic(x, jnp.int32(n.bit_length() - 1))


def _mod_pow2(x, n):
    assert n & (n - 1) == 0
    return jnp.bitwise_and(x, n - 1)


def _sigmoid(x):
    return 1.0 / (1.0 + jnp.exp(-x))


def _rms(x, g):
    return x * lax.rsqrt(jnp.mean(x * x, axis=-1, keepdims=True) + RMS_EPS) * g


def _inproj_kernel(x_ref, nw_ref, w_ref, mu_ref, sprev_ref, cos_ref, sin_ref,
                   w0_ref, w2_ref, a0_ref, a2_ref, g2_ref,
                   r_ref, k_ref, v_ref, dec_ref, a_ref, g_ref, q_ref, ka_ref, va_ref, plast_ref,
                   carry_ref, *, precise, seq_len, tiles_per_seq, cw, cp):
    tm = x_ref.shape[0]
    u = _rms(x_ref[...], nw_ref[...])
    proj = _dot(u, w_ref[...], precise)
    pc = proj[:, :cp]
    row = lax.broadcasted_iota(jnp.int32, (tm, 1), 0)
    prev = pltpu.roll(pc, 1, axis=0)
    if tiles_per_seq is None:
        prev = jnp.where(_mod_pow2(row, seq_len) == 0, sprev_ref[...], prev)
        plast_ref[...] = pc
    else:
        j = pl.program_id(0) % tiles_per_seq
        first = jnp.where(j == 0, sprev_ref[0], carry_ref[...])
        prev = jnp.where(row == 0, first, prev)
        carry_ref[...] = pc[tm - 1:tm, :]
        plast_ref[0] = pc[tm - 1:tm, :]
    pm = pc + (prev - pc) * mu_ref[...]
    r_ref[...] = pm[:, 0:cw]
    k_ref[...] = pm[:, cw:2 * cw]
    v_ref[...] = pm[:, 2 * cw:3 * cw]
    lora = pm[:, 3 * cw:cp]
    zw = w0_ref[...] + _dot(jnp.tanh(lora), w2_ref[...], precise)
    nz = -zw
    softplus = jnp.maximum(nz, 0.0) + jnp.log(1.0 + jnp.exp(-jnp.abs(nz)))
    dec_ref[...] = jnp.exp(-jnp.exp(-softplus - 0.5))
    a_ref[...] = _sigmoid(a0_ref[...] + _dot(lora, a2_ref[...], precise))
    g_ref[...] = _dot(_sigmoid(lora), g2_ref[...], precise)

    cos = jnp.concatenate([cos_ref[...]] * (cw // LANES), axis=1)
    sin = jnp.concatenate([sin_ref[...]] * (cw // LANES), axis=1)
    lane = lax.broadcasted_iota(jnp.int32, (tm, cw), 1)
    lo_half = (lane % HEAD_DIM) < (HEAD_DIM // 2)

    def rope(t):
        partner = jnp.where(lo_half, pltpu.roll(t, cw - HEAD_DIM // 2, axis=1),
                            pltpu.roll(t, HEAD_DIM // 2, axis=1))
        return t * cos + partner * sin

    q_ref[...] = rope(proj[:, cp:cp + cw])
    ka_ref[...] = rope(proj[:, cp + cw:cp + 2 * cw])
    va_ref[...] = proj[:, cp + 2 * cw:cp + 3 * cw]


def _inproj(x2d, sprev, pos_tab, wts, *, precise, seq_len, tm):
    n, d = x2d.shape
    cw, cp = wts["cw"], wts["cp"]
    flat = tm % seq_len == 0 and tm >= seq_len
    nt = n // tm
    tiles_per_seq = None if flat else seq_len // tm
    cos_tab, sin_tab = pos_tab
    ntab = cos_tab.shape[0] // tm
    wdt = F32 if precise else BF16

    const = lambda shape: pl.BlockSpec(shape, lambda i: (0,) * len(shape))
    tok = lambda c: pl.BlockSpec((tm, c), lambda i: (i, 0))
    if flat:
        sprev_spec = pl.BlockSpec((tm, cp), lambda i: (i, 0))
        plast_shape = jax.ShapeDtypeStruct((n, cp), F32)
        plast_spec = pl.BlockSpec((tm, cp), lambda i: (i, 0))
    else:
        sprev_spec = pl.BlockSpec((1, 1, cp), lambda i: (i // tiles_per_seq, 0, 0))
        plast_shape = jax.ShapeDtypeStruct((n // seq_len, 1, cp), F32)
        plast_spec = pl.BlockSpec((1, 1, cp), lambda i: (i // tiles_per_seq, 0, 0))
    in_specs = [
        tok(d), const((1, d)), const((d, cp + 3 * cw)), const((1, cp)), sprev_spec,
        pl.BlockSpec((tm, LANES), lambda i: (i % ntab, 0)),
        pl.BlockSpec((tm, LANES), lambda i: (i % ntab, 0)),
        const((1, cw)), const((LORA_PAD, cw)), const((1, cw)), const((LORA_PAD, cw)),
        const((LORA_PAD, cw)),
    ]
    out_shape = [jax.ShapeDtypeStruct((n, cw), F32)] * 9 + [plast_shape]
    out_specs = [tok(cw)] * 9 + [plast_spec]
    kern = functools.partial(_inproj_kernel, precise=precise, seq_len=seq_len,
                             tiles_per_seq=tiles_per_seq, cw=cw, cp=cp)
    return pl.pallas_call(
        kern, grid=(nt,), in_specs=in_specs, out_specs=out_specs, out_shape=out_shape,
        scratch_shapes=[pltpu.VMEM((1, cp), F32)],
        compiler_params=pltpu.CompilerParams(dimension_semantics=("arbitrary",),
                                             vmem_limit_bytes=VMEM_LIMIT),
        name="inproj_precise" if precise else "inproj",
    )(x2d, wts["norm_mix"], wts["w_cat"].astype(wdt), wts["mu_cat"], sprev, cos_tab, sin_tab,
      wts["w0"], wts["w2p"].astype(wdt), wts["a0"], wts["a2p"].astype(wdt), wts["g2p"].astype(wdt))


def _inproj_cm_kernel(x_ref, nw_ref, wt_ref, wa_ref, mu_ref, sprev_ref, cos_ref, sin_ref,
                      w0_ref, w2_ref, a0_ref, a2_ref, g2_ref,
                      r_ref, k_ref, v_ref, dec_ref, a_ref, g_ref, q_ref, ka_ref, va_ref, plast_ref,
                      carry_ref, *, tiles_per_seq, cw, cp):
    tm = x_ref.shape[0]
    nh = cw // HEAD_DIM
    u = _rms(x_ref[...], nw_ref[...]).astype(BF16)
    pt = _dot_nt(wt_ref[...], u, False)
    j = pl.program_id(0) % tiles_per_seq
    plast_ref[0] = pt[:, tm - LANES:]
    lanes_of = lambda z: jnp.concatenate([z] * (tm // LANES), axis=1)
    cm = lambda z: z.reshape(HEAD_DIM, nh, tm)
    r_ref[...] = cm(pt[0:cw])
    k_ref[...] = cm(pt[cw:2 * cw])
    for i in range(v_ref.shape[1]):
        v_ref[:, i] = cm(pt[2 * cw:3 * cw])
    pl_ = pt[3 * cw:cp]
    before = jnp.where(j == 0, sprev_ref[0], carry_ref[...])
    lane = lax.broadcasted_iota(jnp.int32, (cp - 3 * cw, LANES), 1)
    prev = pltpu.roll(pl_, 1, axis=1)
    head_blk = jnp.where(lane == 0, pltpu.roll(before, 1, axis=1), prev[:, :LANES])
    prev = jnp.concatenate([head_blk, prev[:, LANES:]], axis=1) if tm > LANES else head_blk
    carry_ref[...] = pl_[:, tm - LANES:]
    lora = pl_ + (prev - pl_) * lanes_of(mu_ref[...])
    zw = lanes_of(w0_ref[...]) + _dot(w2_ref[...], jnp.tanh(lora), False)
    nz = -zw
    softplus = jnp.maximum(nz, 0.0) + jnp.log(1.0 + jnp.exp(-jnp.abs(nz)))
    dec_ref[...] = cm(jnp.exp(-jnp.exp(-softplus - 0.5)))
    a_ref[...] = cm(_sigmoid(lanes_of(a0_ref[...]) + _dot(a2_ref[...], lora, False)))
    g_ref[...] = _dot(g2_ref[...], _sigmoid(lora), False).T

    proj = _dot(u, wa_ref[...], False)
    cos = jnp.concatenate([cos_ref[...]] * (cw // LANES), axis=1)
    sin = jnp.concatenate([sin_ref[...]] * (cw // LANES), axis=1)
    lane_c = lax.broadcasted_iota(jnp.int32, (tm, cw), 1)
    lo_half = _mod_pow2(lane_c, HEAD_DIM) < (HEAD_DIM // 2)

    def rope(t):
        partner = jnp.where(lo_half, pltpu.roll(t, cw - HEAD_DIM // 2, axis=1),
                            pltpu.roll(t, HEAD_DIM // 2, axis=1))
        return t * cos + partner * sin

    q_ref[...] = rope(proj[:, 0:cw])
    ka_ref[...] = rope(proj[:, cw:2 * cw])
    va_ref[...] = proj[:, 2 * cw:3 * cw]


def _inproj_cm(x2d, pos_tab, wts, *, b, seq_len, tm):
    n, d = x2d.shape
    cw, cp = wts["cw"], wts["cp"]
    nh = cw // HEAD_DIM
    lp = cp - 3 * cw
    tiles_per_seq = seq_len // tm
    cos_tab, sin_tab = pos_tab
    const = lambda shape: pl.BlockSpec(shape, lambda i: (0,) * len(shape))
    tok = lambda c: pl.BlockSpec((tm, c), lambda i: (i, 0))
    chan = pl.BlockSpec((HEAD_DIM, nh, tm), lambda i: (0, i // tiles_per_seq, i % tiles_per_seq))
    ndup = max(1, LANES // (b * nh))
    chan2 = pl.BlockSpec((HEAD_DIM, ndup, nh, tm), lambda i: (0, 0, i // tiles_per_seq, i % tiles_per_seq))
    per_b = lambda rows: pl.BlockSpec((1, rows, LANES), lambda i: (i // tiles_per_seq, 0, 0))
    in_specs = [
        tok(d), const((1, d)), const((cp, d)), const((d, 3 * cw)), const((lp, LANES)), per_b(lp),
        pl.BlockSpec((tm, LANES), lambda i: (i % tiles_per_seq, 0)),
        pl.BlockSpec((tm, LANES), lambda i: (i % tiles_per_seq, 0)),
        const((cw, LANES)), const((cw, LORA_PAD)), const((cw, LANES)), const((cw, LORA_PAD)),
        const((cw, LORA_PAD)),
    ]
    chan_shape = jax.ShapeDtypeStruct((HEAD_DIM, b * nh, seq_len), F32)
    chan2_shape = jax.ShapeDtypeStruct((HEAD_DIM, ndup, b * nh, seq_len), F32)
    out_shape = [chan_shape, chan_shape, chan2_shape, chan_shape, chan_shape] + [
        jax.ShapeDtypeStruct((n, cw), F32)] * 4 + [jax.ShapeDtypeStruct((b, cp, LANES), F32)]
    out_specs = [chan, chan, chan2, chan, chan] + [tok(cw)] * 4 + [per_b(cp)]
    perm = lambda z: jnp.transpose(z.reshape(z.shape[0], -1, nh, HEAD_DIM), (0, 1, 3, 2)).reshape(z.shape)
    rp = wts["rp"]
    w = wts["w_cat"]
    wt = jnp.concatenate([perm(w[:, :3 * cw]), w[:, 3 * cw:cp]], axis=1).T.astype(BF16)
    col = lambda z: jnp.broadcast_to(z.reshape(-1, 1), (z.size, LANES))
    zero_shift = jnp.zeros((b, lp, LANES), F32)
    outs = pl.pallas_call(
        functools.partial(_inproj_cm_kernel, tiles_per_seq=tiles_per_seq, cw=cw, cp=cp),
        grid=(n // tm,), in_specs=in_specs, out_specs=out_specs, out_shape=out_shape,
        scratch_shapes=[pltpu.VMEM((lp, LANES), F32)],
        compiler_params=pltpu.CompilerParams(dimension_semantics=("arbitrary",),
                                             vmem_limit_bytes=VMEM_LIMIT),
        name="inproj",
    )(x2d, wts["norm_mix"], wt, w[:, cp:].astype(BF16), col(wts["mu_cat"][:, 3 * cw:]), zero_shift,
      cos_tab, sin_tab,
      col(perm(wts["w0"])), perm(wts["w2p"]).T.astype(BF16), col(perm(wts["a0"])),
      perm(wts["a2p"]).T.astype(BF16), wts["g2p"].T.astype(BF16))
    last = outs[9][:, :, LANES - 1]
    unperm = lambda z: jnp.transpose(z.reshape(z.shape[0], -1, HEAD_DIM, nh), (0, 1, 3, 2)).reshape(z.shape)
    plast = jnp.concatenate([unperm(last[:, :3 * cw]), last[:, 3 * cw:rp]], axis=1)
    return list(outs[:9]) + [plast]


def _lane_fold(x, kpar):
    s = LANES // 2
    while s >= LANES // kpar:
        x = x + pltpu.roll(x, s, axis=x.ndim - 1)
        s //= 2
    return x


def _wkv_kernel(*refs, kpar, shift):
    if shift:
        (r_ref, k_ref, w_ref, a_ref, v_ref, mur_ref, muk_ref, muv_ref, p0r_ref, p0k_ref, p0v_ref,
         kk_ref, ka_ref, rk_ref, lnw_ref, lnb_ref, s0_ref,
         y_ref, s_ref, aop_ref, wr_ref, bop_ref, km_ref, sc_ref, vs_ref, cr_ref, ck_ref, cv_ref) = refs
    else:
        (r_ref, k_ref, w_ref, a_ref, v_ref, kk_ref, ka_ref, rk_ref, lnw_ref, lnb_ref, s0_ref,
         y_ref, s_ref, aop_ref, wr_ref, bop_ref, km_ref, sc_ref, vs_ref) = refs
    tt, nr, _ = r_ref.shape
    nv = v_ref.shape[1]
    nvb = nv // SUBLANES

    @pl.when(pl.program_id(1) == 0)
    def _():
        s_ref[...] = s0_ref[...]
        if shift:
            cr_ref[...] = p0r_ref[...]
            ck_ref[...] = p0k_ref[...]
            cv_ref[...] = p0v_ref[...]

    def shifted(x_ref, c_ref, mu_ref):
        raw = x_ref[...]
        prev = jnp.concatenate([c_ref[...][None], raw[:tt - 1]], axis=0) if tt > 1 else c_ref[...][None]
        c_ref[...] = raw[tt - 1]
        return raw + (prev - raw) * mu_ref[...]

    if shift:
        r = shifted(r_ref, cr_ref, mur_ref)
        k = shifted(k_ref, ck_ref, muk_ref)
        vs_ref[...] = shifted(v_ref, cv_ref, muv_ref)
    else:
        r = r_ref[...]
        k = k_ref[...]
        vs_ref[...] = v_ref[...]
    a = a_ref[...]
    kkr = k * kk_ref[...]
    ss = _lane_fold(jnp.sum(kkr * kkr, axis=1, keepdims=True), kpar)
    kkn = kkr / jnp.maximum(jnp.sqrt(ss), 1e-12)
    bop = kkn * a
    km = k * (1.0 + (a - 1.0) * ka_ref[...])
    aop_ref[...] = -kkn
    bop_ref[...] = bop
    km_ref[...] = km
    wr_ref[...] = w_ref[...] * r
    br = _lane_fold(jnp.sum(bop * r, axis=1, keepdims=True), kpar)
    kr = _lane_fold(jnp.sum(km * r, axis=1, keepdims=True), kpar)
    bonus = _lane_fold(jnp.sum(r * km * rk_ref[...], axis=1, keepdims=True), kpar)
    sc_ref[:, 0:1, :] = br
    sc_ref[:, 1:2, :] = kr

    def step(t, carry):
        vv = [vs_ref[t, vb * SUBLANES:(vb + 1) * SUBLANES, :] for vb in range(nvb)]
        acc_sa = [None] * nvb
        acc_y = [None] * nvb
        for kr_i in range(nr):
            a_row = aop_ref[t, kr_i:kr_i + 1, :]
            wr_row = wr_ref[t, kr_i:kr_i + 1, :]
            for vb in range(nvb):
                blk = s_ref[kr_i, vb * SUBLANES:(vb + 1) * SUBLANES, :]
                pa = blk * a_row
                py = blk * wr_row
                acc_sa[vb] = pa if acc_sa[vb] is None else acc_sa[vb] + pa
                acc_y[vb] = py if acc_y[vb] is None else acc_y[vb] + py
        sa = [_lane_fold(x, kpar) for x in acc_sa]
        yp = [_lane_fold(x, kpar) for x in acc_y]
        for kr_i in range(nr):
            w_row = w_ref[t, kr_i:kr_i + 1, :]
            b_row = bop_ref[t, kr_i:kr_i + 1, :]
            k_row = km_ref[t, kr_i:kr_i + 1, :]
            for vb in range(nvb):
                sl = slice(vb * SUBLANES, (vb + 1) * SUBLANES)
                s_ref[kr_i, sl, :] = s_ref[kr_i, sl, :] * w_row + sa[vb] * b_row + vv[vb] * k_row
        br_row = sc_ref[t, 0:1, :]
        kr_row = sc_ref[t, 1:2, :]
        for vb in range(nvb):
            y_ref[t, vb * SUBLANES:(vb + 1) * SUBLANES, :] = yp[vb] + sa[vb] * br_row + vv[vb] * kr_row
        return carry

    lax.fori_loop(0, tt, step, 0)

    y = y_ref[...]
    mean = jnp.mean(y, axis=1, keepdims=True)
    yc = y - mean
    var = jnp.mean(yc * yc, axis=1, keepdims=True)
    y_ref[...] = yc * lax.rsqrt(var + LNX_EPS) * lnw_ref[...] + lnb_ref[...] + bonus * vs_ref[...]


def _to_lane_layout(x, b, t, h):
    return jnp.transpose(x.reshape(b, t, h, HEAD_DIM), (1, 3, 0, 2)).reshape(t, HEAD_DIM, b * h)


def _wkv(r, k, v, w, a, state, prm, *, b, t, h, tt, shift=None):
    p = b * h
    if p >= LANES:
        assert p % LANES == 0
        g, kpar = p // LANES, 1
    else:
        assert LANES % p == 0
        g, kpar = 1, LANES // p
    nr = HEAD_DIM // kpar

    def krows(x):
        if kpar > 1:
            return x.reshape(1, x.shape[0], nr, LANES)
        return jnp.transpose(x.reshape(x.shape[0], HEAD_DIM, g, LANES), (2, 0, 1, 3))

    def vrows(x):
        if kpar > 1:
            return x[None] if x.shape[-1] == LANES else jnp.concatenate([x] * kpar, axis=-1)[None]
        return jnp.transpose(x.reshape(x.shape[0], HEAD_DIM, g, LANES), (2, 0, 1, 3))

    rl, kl, wl, al = (krows(z) for z in (r, k, w, a))
    vl = vrows(v)

    def prm_k(z):
        x = jnp.broadcast_to(z.T[:, None, :], (HEAD_DIM, b, h)).reshape(1, HEAD_DIM, p)
        return krows(x)[:, 0]

    def prm_v(z):
        x = jnp.broadcast_to(z.T[:, None, :], (HEAD_DIM, b, h)).reshape(1, HEAD_DIM, p)
        return vrows(x)[:, 0]

    st = jnp.transpose(state.reshape(p, HEAD_DIM, HEAD_DIM), (2, 1, 0))
    if kpar > 1:
        st = jnp.transpose(st.reshape(nr, kpar, HEAD_DIM, p), (0, 2, 1, 3)).reshape(1, nr, HEAD_DIM, LANES)
    else:
        st = jnp.transpose(st.reshape(HEAD_DIM, HEAD_DIM, g, LANES), (2, 0, 1, 3))

    nt = t // tt
    tile_k = pl.BlockSpec((None, tt, nr, LANES), lambda gi, ti: (gi, ti, 0, 0))
    tile_v = pl.BlockSpec((None, tt, HEAD_DIM, LANES), lambda gi, ti: (gi, ti, 0, 0))
    par_k = pl.BlockSpec((None, nr, LANES), lambda gi, ti: (gi, 0, 0))
    par_v = pl.BlockSpec((None, HEAD_DIM, LANES), lambda gi, ti: (gi, 0, 0))
    st_spec = pl.BlockSpec((None, nr, HEAD_DIM, LANES), lambda gi, ti: (gi, 0, 0, 0))
    shift_args, shift_specs, shift_scratch = [], [], []
    if shift is not None:
        mu_r, mu_k, mu_v, p0_r, p0_k, p0_v = shift
        lane0 = lambda z: _to_lane_layout(z, b, 1, h)
        shift_args = [prm_k(mu_r), prm_k(mu_k), prm_v(mu_v),
                      krows(lane0(p0_r))[:, 0], krows(lane0(p0_k))[:, 0], vrows(lane0(p0_v))[:, 0]]
        shift_specs = [par_k, par_k, par_v, par_k, par_k, par_v]
        shift_scratch = [pltpu.VMEM((nr, LANES), F32)] * 2 + [pltpu.VMEM((HEAD_DIM, LANES), F32)]
    y, s_fin = pl.pallas_call(
        functools.partial(_wkv_kernel, kpar=kpar, shift=shift is not None),
        grid=(g, nt),
        in_specs=[tile_k, tile_k, tile_k, tile_k, tile_v] + shift_specs
                 + [par_k, par_k, par_k, par_v, par_v, st_spec],
        out_specs=[tile_v, st_spec],
        out_shape=[jax.ShapeDtypeStruct((g, t, HEAD_DIM, LANES), F32),
                   jax.ShapeDtypeStruct((g, nr, HEAD_DIM, LANES), F32)],
        scratch_shapes=[pltpu.VMEM((tt, nr, LANES), F32)] * 4 + [pltpu.VMEM((tt, SUBLANES, LANES), F32),
                                                                 pltpu.VMEM((tt, HEAD_DIM, LANES), F32)]
                       + shift_scratch,
        compiler_params=pltpu.CompilerParams(dimension_semantics=("arbitrary", "arbitrary"),
                                             vmem_limit_bytes=VMEM_LIMIT),
        name="wkv",
    )(rl, kl, wl, al, vl, *shift_args, prm_k(prm["k_k"]), prm_k(prm["k_a"]), prm_k(prm["r_k"]),
      prm_v(prm["lnx_w"]), prm_v(prm["lnx_b"]), st)

    if kpar > 1:
        yt = y[0, :, :, :p]
        sf = jnp.transpose(s_fin[0].reshape(nr, HEAD_DIM, kpar, p), (3, 1, 0, 2))
    else:
        yt = jnp.transpose(y, (1, 2, 0, 3)).reshape(t, HEAD_DIM, p)
        sf = jnp.transpose(s_fin, (0, 3, 2, 1)).reshape(p, HEAD_DIM, HEAD_DIM)
    y_tok = jnp.transpose(yt.reshape(t, HEAD_DIM, b, h), (2, 0, 3, 1)).reshape(b * t, h * HEAD_DIM)
    return y_tok, sf.reshape(b, h, HEAD_DIM, HEAD_DIM)


def _attn_prompt_kernel(q_ref, k_ref, v_ref, o_ref, ob_ref, lb_ref, qs_ref, vs_ref, kt_ref,
                        bias2_ref, bias1_ref, *, unroll):
    s_len = q_ref.shape[0]
    blk = 128
    nchunk = s_len // blk
    lane = lax.broadcasted_iota(jnp.int32, (blk, LANES), 1)
    head0 = lane < HEAD_DIM
    head1 = jnp.logical_not(head0)
    qi = _mod_pow2(lax.broadcasted_iota(jnp.int32, (2 * blk, 2 * blk), 0), blk)
    kj = lax.broadcasted_iota(jnp.int32, (2 * blk, 2 * blk), 1)
    bias2_ref[...] = jnp.where(kj < blk, jnp.where(kj >= qi, 0.0, NEG_INF),
                               jnp.where(kj - blk <= qi, 0.0, NEG_INF))
    qi1 = _mod_pow2(lax.broadcasted_iota(jnp.int32, (2 * blk, blk), 0), blk)
    kj1 = lax.broadcasted_iota(jnp.int32, (2 * blk, blk), 1)
    bias1_ref[...] = jnp.where(kj1 <= qi1, 0.0, NEG_INF)
    scale = 1.0 / math.sqrt(HEAD_DIM)

    def finish(bi, rows, s, v0, v1):
        s0, s1 = s[:blk], s[blk:]
        m0 = jnp.max(s0, axis=1, keepdims=True)
        m1 = jnp.max(s1, axis=1, keepdims=True)
        a0 = jnp.dot(jnp.exp(s0 - m0).astype(BF16), v0, preferred_element_type=F32)
        a1 = jnp.dot(jnp.exp(s1 - m1).astype(BF16), v1, preferred_element_type=F32)
        num = jnp.where(head0, a0, a1)
        den = pltpu.roll(jnp.where(head0, a1, a0), HEAD_DIM, axis=1)
        ob_ref[bi, rows, :] = num / den
        lb_ref[bi, rows, :] = jnp.where(head0, m0, m1) + jnp.log(den)

    for bi, (window, dil) in enumerate(DILATED):
        assert window // dil == blk
        unit = blk * dil
        nblk = s_len // unit

        def pos_rows(j, dil=dil, unit=unit, nblk=nblk):
            if dil == 1:
                return pl.ds(pl.multiple_of(j * blk, blk), blk)
            return pl.ds((j % nblk) * unit + j // nblk, blk, stride=dil)

        def stage(j, carry, pos_rows=pos_rows):
            rows = pos_rows(j)
            dst = pl.ds(pl.multiple_of(j * blk, blk), blk)
            q = q_ref[rows, :] * scale
            v = v_ref[rows, :]
            qs_ref[0, dst, :] = jnp.where(head0, q, 0.0).astype(BF16)
            qs_ref[1, dst, :] = jnp.where(head1, q, 0.0).astype(BF16)
            vs_ref[0, dst, :] = jnp.where(head0, v, 1.0).astype(BF16)
            vs_ref[1, dst, :] = jnp.where(head1, v, 1.0).astype(BF16)
            kt_ref[:, dst] = k_ref[rows, :].T.astype(BF16)
            return carry

        def first(c, carry, bi=bi, nblk=nblk, pos_rows=pos_rows):
            j = c * nblk
            src = pl.ds(pl.multiple_of(j * blk, blk), blk)
            qs = jnp.concatenate([qs_ref[0, src, :], qs_ref[1, src, :]], axis=0)
            s = jnp.dot(qs, kt_ref[:, src], preferred_element_type=F32) + bias1_ref[...]
            finish(bi, pos_rows(j), s, vs_ref[0, src, :], vs_ref[1, src, :])
            return carry

        def later(idx, carry, bi=bi, nblk=nblk, pos_rows=pos_rows):
            j = (idx // (nblk - 1)) * nblk + 1 + idx % (nblk - 1)
            src = pl.ds(pl.multiple_of(j * blk, blk), blk)
            keys = pl.ds(pl.multiple_of((j - 1) * blk, blk), 2 * blk)
            qs = jnp.concatenate([qs_ref[0, src, :], qs_ref[1, src, :]], axis=0)
            s = jnp.dot(qs, kt_ref[:, keys], preferred_element_type=F32) + bias2_ref[...]
            finish(bi, pos_rows(j), s, vs_ref[0, keys, :], vs_ref[1, keys, :])
            return carry

        lax.fori_loop(0, nchunk, stage, 0, unroll=2)
        lax.fori_loop(0, dil, first, 0, unroll=min(dil, unroll))
        if nblk > 1:
            lax.fori_loop(0, dil * (nblk - 1), later, 0, unroll=unroll)

    def merge(ci, carry):
        sl = pl.ds(pl.multiple_of(ci * blk, blk), blk)
        l0, l1, l2 = lb_ref[0, sl, :], lb_ref[1, sl, :], lb_ref[2, sl, :]
        m = jnp.maximum(jnp.maximum(l0, l1), l2)
        w0, w1, w2 = jnp.exp(l0 - m), jnp.exp(l1 - m), jnp.exp(l2 - m)
        tot = w0 + w1 + w2
        o_ref[sl, :] = (w0 * ob_ref[0, sl, :] + w1 * ob_ref[1, sl, :] + w2 * ob_ref[2, sl, :]) / tot
        return carry

    lax.fori_loop(0, s_len // blk, merge, 0)


def _attn_prompt(q, k, v, *, b, s):
    cw = q.shape[1]
    hp = cw // LANES
    assert s % (128 * DILATED[-1][1]) == 0
    spec = pl.BlockSpec((None, s, LANES), lambda bi, hi: (bi, 0, hi))
    out = pl.pallas_call(
        functools.partial(_attn_prompt_kernel, unroll=ATTN_UNROLL), grid=(b, hp),
        in_specs=[spec, spec, spec], out_specs=spec,
        out_shape=jax.ShapeDtypeStruct((b, s, cw), F32),
        scratch_shapes=[pltpu.VMEM((3, s, LANES), F32), pltpu.VMEM((3, s, LANES), F32),
                        pltpu.VMEM((2, s, LANES), BF16), pltpu.VMEM((2, s, LANES), BF16),
                        pltpu.VMEM((LANES, s), BF16),
                        pltpu.VMEM((256, 256), F32), pltpu.VMEM((256, 128), F32)],
        compiler_params=pltpu.CompilerParams(dimension_semantics=("arbitrary", "arbitrary"),
                                             vmem_limit_bytes=VMEM_LIMIT),
        name="attn_prompt",
    )(q.reshape(b, s, cw), k.reshape(b, s, cw), v.reshape(b, s, cw))
    return out.reshape(b * s, cw)


def _attn_decode_kernel(q_ref, kn_ref, vn_ref, kc_ref, vc_ref, o_ref, *, t_new):
    wc, cw = kc_ref.shape
    nh = cw // HEAD_DIM
    npad = kn_ref.shape[0]
    rows = nh * t_new
    q = q_ref[...]
    qt = jnp.concatenate([q] * nh, axis=0)
    rh = _div_pow2(lax.broadcasted_iota(jnp.int32, (rows, cw), 0), t_new)
    lh = _div_pow2(lax.broadcasted_iota(jnp.int32, (rows, cw), 1), HEAD_DIM)
    own_head = rh == lh
    qbd = jnp.where(own_head, qt, 0.0)
    scale = 1.0 / math.sqrt(HEAD_DIM)
    s_c = _dot3(qbd, kc_ref[...], nt=True) * scale
    s_n = _dot3(qbd, kn_ref[...], nt=True) * scale
    t_c = _mod_pow2(lax.broadcasted_iota(jnp.int32, (rows, wc), 0), t_new)
    dist_c = wc + t_c - lax.broadcasted_iota(jnp.int32, (rows, wc), 1)
    t_n = _mod_pow2(lax.broadcasted_iota(jnp.int32, (rows, npad), 0), t_new)
    j_n = lax.broadcasted_iota(jnp.int32, (rows, npad), 1)
    dist_n = t_n - j_n
    ecs, ens, ls, lses = [], [], [], []
    for window, dil in DILATED:
        ok_c = jnp.logical_and(_mod_pow2(dist_c, dil) == 0, dist_c <= window)
        ok_n = jnp.logical_and(jnp.logical_and(dist_n >= 0, j_n < t_new),
                               jnp.logical_and(_mod_pow2(dist_n, dil) == 0, dist_n <= window))
        sc = jnp.where(ok_c, s_c, NEG_INF)
        sn = jnp.where(ok_n, s_n, NEG_INF)
        m = jnp.maximum(jnp.max(sc, axis=1, keepdims=True), jnp.max(sn, axis=1, keepdims=True))
        ec = jnp.exp(sc - m)
        en = jnp.exp(sn - m)
        l = jnp.sum(ec, axis=1, keepdims=True) + jnp.sum(en, axis=1, keepdims=True)
        ecs.append(ec)
        ens.append(en)
        ls.append(l)
        lses.append(m + jnp.log(l))
    o_all = (_dot3(jnp.concatenate(ecs, axis=0), vc_ref[...])
             + _dot3(jnp.concatenate(ens, axis=0), vn_ref[...]))
    outs = [o_all[i * rows:(i + 1) * rows] / ls[i] for i in range(len(DILATED))]
    m = jnp.maximum(jnp.maximum(lses[0], lses[1]), lses[2])
    ws = [jnp.exp(x - m) for x in lses]
    full = (ws[0] * outs[0] + ws[1] * outs[1] + ws[2] * outs[2]) / (ws[0] + ws[1] + ws[2])
    full = jnp.where(own_head, full, 0.0).reshape(nh, t_new, cw)
    o_ref[...] = jnp.sum(full, axis=0)


def _attn_decode(q, k, v, cache_k, cache_v, *, b, t):
    cw = q.shape[1]
    wc = cache_k.shape[1]
    npad = LANES
    pad = lambda z: jnp.pad(z.reshape(b, t, cw), ((0, 0), (0, npad - t), (0, 0)))
    new_spec = pl.BlockSpec((None, npad, cw), lambda bi: (bi, 0, 0))
    cache_spec = pl.BlockSpec((None, wc, cw), lambda bi: (bi, 0, 0))
    tok_spec = pl.BlockSpec((None, t, cw), lambda bi: (bi, 0, 0))
    out = pl.pallas_call(
        functools.partial(_attn_decode_kernel, t_new=t), grid=(b,),
        in_specs=[tok_spec, new_spec, new_spec, cache_spec, cache_spec], out_specs=tok_spec,
        out_shape=jax.ShapeDtypeStruct((b, t, cw), F32),
        compiler_params=pltpu.CompilerParams(dimension_semantics=("arbitrary",),
                                             vmem_limit_bytes=VMEM_LIMIT),
        name="attn_decode",
    )(q.reshape(b, t, cw), pad(k), pad(v), cache_k, cache_v)
    return out.reshape(b * t, cw)


def _route(logits):
    lane_i = lax.broadcasted_iota(jnp.int32, logits.shape, 1)
    lane = lane_i.astype(F32)
    is_g = lane_i < N_GROUPS
    lg = jnp.where(is_g, logits, NEG_INF)
    mg = jnp.max(lg, axis=1, keepdims=True)
    pg_sel = 1.0 / jnp.sum(jnp.where(is_g, jnp.exp(lg - mg), 0.0), axis=1, keepdims=True)
    gsel = jnp.min(jnp.where(lg == mg, lane, float(LANES)), axis=1, keepdims=True)
    e_lane = lane_i - N_GROUPS
    e_group = _div_pow2(e_lane, EXPERTS_PER_GROUP).astype(F32)
    in_sel = jnp.logical_and(jnp.logical_and(e_lane >= 0, e_lane < N_EXPERTS), e_group == gsel)
    le = jnp.where(in_sel, logits, NEG_INF)
    me = jnp.max(le, axis=1, keepdims=True)
    ee = jnp.where(in_sel, jnp.exp(le - me), 0.0)
    pe = jnp.where(in_sel, ee / jnp.sum(ee, axis=1, keepdims=True), -1.0)
    p1 = jnp.max(pe, axis=1, keepdims=True)
    i1 = jnp.min(jnp.where(pe == p1, lane, float(LANES)), axis=1, keepdims=True)
    pe2 = jnp.where(lane == i1, -1.0, pe)
    p2 = jnp.max(pe2, axis=1, keepdims=True)
    i2 = jnp.min(jnp.where(pe2 == p2, lane, float(LANES)), axis=1, keepdims=True)
    tot = p1 + p2
    return jnp.where(lane == i1, pg_sel * p1 / tot, 0.0) + jnp.where(lane == i2, pg_sel * p2 / tot, 0.0)


def _post_kernel(x_ref, rw_ref, g_ref, att_ref, p_ref, an_ref, wo_ref, nf_ref, wr_ref, br_ref,
                 wg_ref, wu_ref, wd_ref, np_ref, wpg_ref, wpp_ref, nfin_ref,
                 y_ref, h1_ref, u2_ref, gate_ref, acc_ref, *, precise):
    e = pl.program_id(1)
    cw = rw_ref.shape[1]
    ne_step, ff, _ = wd_ref.shape

    @pl.when(e == 0)
    def _():
        mix_r = rw_ref[...] * g_ref[...]
        mix_a = _rms(att_ref[...], an_ref[...])
        h1 = x_ref[...] + _dot(mix_r, wo_ref[0:cw, :], precise) + _dot(mix_a, wo_ref[cw:2 * cw, :], precise)
        h1_ref[...] = h1
        u2 = _rms(h1, nf_ref[...])
        u2_ref[...] = u2.astype(u2_ref.dtype)
        gate_ref[...] = _route(_dot(u2, wr_ref[...], True) + br_ref[...])
        acc_ref[...] = jnp.zeros_like(acc_ref)

    lane = lax.broadcasted_iota(jnp.int32, gate_ref.shape, 1)
    u2 = u2_ref[...]
    gate = gate_ref[...]
    hidden = []
    for i in range(ne_step):
        ge = jnp.sum(jnp.where(lane == e * ne_step + i + N_GROUPS, gate, 0.0), axis=1, keepdims=True)
        gp = _dot(u2, wg_ref[i], precise)
        up = _dot(u2, wu_ref[i], precise)
        hidden.append((gp * _sigmoid(gp) * up * ge).astype(u2.dtype))
    acc_ref[...] += _dot(jnp.concatenate(hidden, axis=1), wd_ref[...].reshape(ne_step * ff, -1), precise)

    @pl.when(e == pl.num_programs(1) - 1)
    def _():
        h2 = h1_ref[...] + acc_ref[...]
        u3 = _rms(h2, np_ref[...])
        h3 = h2 + _sigmoid(_dot(u3, wpg_ref[...], precise)) * _dot(p_ref[...], wpp_ref[...], precise)
        y_ref[...] = _rms(h3, nfin_ref[...])


def _post(x2d, rw, g, att, p2d, wts, *, precise, tm, ne_step):
    n, d = x2d.shape
    cw = rw.shape[1]
    pd = p2d.shape[1]
    ne, _, ff = wts["w_g"].shape
    assert ne % ne_step == 0
    wdt = F32 if precise else BF16
    tok = lambda c: pl.BlockSpec((tm, c), lambda i, e: (i, 0))
    const = lambda shape: pl.BlockSpec(shape, lambda i, e: (0,) * len(shape))
    in_specs = [
        tok(d), tok(cw), tok(cw), tok(cw), tok(pd),
        const((1, cw)), const((2 * cw, d)), const((1, d)), const((d, LANES)), const((1, LANES)),
        pl.BlockSpec((ne_step, d, ff), lambda i, e: (e, 0, 0)),
        pl.BlockSpec((ne_step, d, ff), lambda i, e: (e, 0, 0)),
        pl.BlockSpec((ne_step, ff, d), lambda i, e: (e, 0, 0)),
        const((1, d)), const((d, d)), const((pd, d)), const((1, d)),
    ]
    return pl.pallas_call(
        functools.partial(_post_kernel, precise=precise),
        grid=(n // tm, ne // ne_step), in_specs=in_specs, out_specs=tok(d),
        out_shape=jax.ShapeDtypeStruct((n, d), F32),
        scratch_shapes=[pltpu.VMEM((tm, d), F32), pltpu.VMEM((tm, d), wdt),
                        pltpu.VMEM((tm, LANES), F32), pltpu.VMEM((tm, d), F32)],
        compiler_params=pltpu.CompilerParams(dimension_semantics=("arbitrary", "arbitrary"),
                                             vmem_limit_bytes=VMEM_LIMIT),
        name="post_precise" if precise else "post",
    )(x2d, rw, g, att, p2d, wts["attn_out_norm"], wts["w_out"].astype(wdt), wts["norm_ffn"],
      wts["w_router"], wts["b_router"], wts["w_g"].astype(wdt), wts["w_u"].astype(wdt),
      wts["w_d"].astype(wdt),
      wts["norm_ple"], wts["ple_gate_w"].astype(wdt), wts["ple_proj"].astype(wdt), wts["norm_final"])


def _rope_tables(pos):
    half = HEAD_DIM // 2
    inv = np.power(ROPE_THETA, -np.arange(half, dtype=np.float64) * 2.0 / HEAD_DIM)
    ang = np.asarray(pos, np.float64)[:, None] * inv[None, :]
    cos, sin = np.cos(ang), np.sin(ang)
    cos_h = np.concatenate([cos, cos], axis=1)
    sin_h = np.concatenate([-sin, sin], axis=1)
    reps = LANES // HEAD_DIM
    return (jnp.asarray(np.concatenate([cos_h] * reps, axis=1), F32),
            jnp.asarray(np.concatenate([sin_h] * reps, axis=1), F32))


def _layer_weights(i, norm_mix, w_in, mu_shift, decay_w0, decay_w2, iclr_a0, iclr_a2, gate_g2,
                   k_k, k_a, r_k, lnx_w, lnx_b, attn_out_norm, w_out, norm_ffn,
                   router_group_w, router_group_b, router_expert_w, router_expert_b,
                   expert_w_gate, expert_w_up, expert_w_down, norm_ple, ple_gate_w, ple_proj, norm_final):
    d = w_in.shape[1]
    cw = decay_w0.shape[1]
    n_lora = DECAY_LORA + AAA_LORA + GATE_LORA
    rp = 3 * cw + n_lora
    cp = 3 * cw + LORA_PAD
    padc = lambda z: jnp.pad(z, ((0, 0), (0, LORA_PAD - n_lora)))
    w = w_in[i]
    w_cat = jnp.concatenate([padc(w[:, :rp]), w[:, rp:]], axis=1)
    padr = lambda z, lo: jnp.pad(z, ((lo, LORA_PAD - lo - z.shape[0]), (0, 0)))
    n_route = N_GROUPS + N_EXPERTS
    heads = cw // HEAD_DIM
    return dict(
        cw=cw, cp=cp, rp=rp,
        norm_mix=norm_mix[i][None], w_cat=w_cat, mu_cat=padc(mu_shift[i][None]),
        w0=decay_w0[i][None], w2p=padr(decay_w2[i], 0), a0=iclr_a0[i][None],
        a2p=padr(iclr_a2[i], DECAY_LORA), g2p=padr(gate_g2[i], DECAY_LORA + AAA_LORA),
        k_k=k_k[i].reshape(heads, HEAD_DIM), k_a=k_a[i].reshape(heads, HEAD_DIM), r_k=r_k[i],
        lnx_w=lnx_w[i].reshape(heads, HEAD_DIM), lnx_b=lnx_b[i].reshape(heads, HEAD_DIM),
        attn_out_norm=attn_out_norm[i][None], w_out=w_out[i], norm_ffn=norm_ffn[i][None],
        w_router=jnp.pad(jnp.concatenate([router_group_w[i], router_expert_w[i]], axis=1),
                         ((0, 0), (0, LANES - n_route))),
        b_router=jnp.pad(jnp.concatenate([router_group_b[i], router_expert_b[i]])[None],
                         ((0, 0), (0, LANES - n_route))),
        w_g=expert_w_gate[i], w_u=expert_w_up[i], w_d=expert_w_down[i],
        norm_ple=norm_ple[i][None], ple_gate_w=ple_gate_w[i], ple_proj=ple_proj[i],
        norm_final=norm_final[None],
    )


def _unpad_shift(pl_rows, wts):
    return pl_rows[:, :wts["rp"]]


def kernel(x_prompt, x_sample, cache_k_win, cache_v_win, state_wkv, state_shift, p_prompt, p_sample, norm_mix, w_in, mu_shift, decay_w0, decay_w2, iclr_a0, iclr_a2, gate_g2, k_k, k_a, r_k, lnx_w, lnx_b, attn_out_norm, w_out, norm_ffn, router_group_w, router_group_b, router_expert_w, router_expert_b, expert_w_gate, expert_w_up, expert_w_down, norm_ple, ple_gate_w, ple_proj, norm_final):
    b, s, d = x_prompt.shape
    db, t, _ = x_sample.shape
    depth = w_in.shape[0]
    assert depth == 1, "a deeper stack would chain the layer below over h"
    wts = _layer_weights(0, norm_mix, w_in, mu_shift, decay_w0, decay_w2, iclr_a0, iclr_a2, gate_g2,
                         k_k, k_a, r_k, lnx_w, lnx_b, attn_out_norm, w_out, norm_ffn,
                         router_group_w, router_group_b, router_expert_w, router_expert_b,
                         expert_w_gate, expert_w_up, expert_w_down, norm_ple, ple_gate_w, ple_proj,
                         norm_final)
    cw, cp, rp = wts["cw"], wts["cp"], wts["rp"]
    heads = cw // HEAD_DIM
    keep = min(WIN_MAX, s)
    padc = lambda z: jnp.pad(z, ((0, 0), (0, cp - rp)))

    tm_p = min(512, s)
    pos_p = _rope_tables(np.arange(s))
    r, k, v, w, a, g, q, ka, va, plast = _inproj_cm(
        x_prompt.reshape(b * s, d), pos_p, wts, b=b, seq_len=s, tm=tm_p)
    zero_wkv = jnp.zeros((b, heads, HEAD_DIM, HEAD_DIM), F32)
    lane_p = lambda z: z.reshape(-1, s).T.reshape(s, HEAD_DIM, -1)
    mu = wts["mu_cat"][0, :3 * cw].reshape(3, heads, HEAD_DIM)
    zero_p = jnp.zeros((b, cw), F32)
    rw, prompt_wkv = _wkv(lane_p(r), lane_p(k), lane_p(v), lane_p(w), lane_p(a), zero_wkv, wts,
                          b=b, t=s, h=heads, tt=min(64, s),
                          shift=(mu[0], mu[1], mu[2], zero_p, zero_p, zero_p))
    att = _attn_prompt(q, ka, va, b=b, s=s)
    y_prompt = _post(x_prompt.reshape(b * s, d), rw, g, att, p_prompt[0].reshape(b * s, -1), wts,
                     precise=False, tm=min(512, b * s), ne_step=4).reshape(b, s, d)
    prompt_k_win = ka.reshape(b, s, heads, HEAD_DIM)[:, s - keep:][None]
    prompt_v_win = va.reshape(b, s, heads, HEAD_DIM)[:, s - keep:][None]
    prompt_shift = plast[None]

    n_s = db * t
    pos_s = _rope_tables(np.tile(PAST_LEN + np.arange(t), db))
    sprev = jnp.repeat(padc(state_shift[0]), t, axis=0)
    r, k, v, w, a, g, q, ka, va, pfull = _inproj(
        x_sample.reshape(n_s, d), sprev, pos_s, wts, precise=True, seq_len=t, tm=n_s)
    lane_s = lambda z: _to_lane_layout(z, db, t, heads)
    rw, sample_wkv = _wkv(lane_s(r), lane_s(k), lane_s(v), lane_s(w), lane_s(a), state_wkv[0], wts,
                          b=db, t=t, h=heads, tt=t)
    wc = cache_k_win.shape[2]
    att = _attn_decode(q, ka, va, cache_k_win[0].reshape(db, wc, cw), cache_v_win[0].reshape(db, wc, cw),
                       b=db, t=t)
    y_sample = _post(x_sample.reshape(n_s, d), rw, g, att, p_sample[0].reshape(n_s, -1), wts,
                     precise=True, tm=n_s, ne_step=2).reshape(db, t, d)
    sample_k_rows = ka.reshape(db, t, heads, HEAD_DIM)[None]
    sample_v_rows = va.reshape(db, t, heads, HEAD_DIM)[None]
    sample_shift = _unpad_shift(pfull.reshape(db, t, cp)[:, t - 1], wts)[None]

    return (y_prompt, y_sample, prompt_k_win, prompt_v_win, prompt_wkv[None], prompt_shift,
            sample_k_rows, sample_v_rows, sample_wkv[None], sample_shift)
```

```python
import functools
import math

import jax
import jax.numpy as jnp
import numpy as np
from jax import lax
from jax.experimental import pallas as pl
from jax.experimental.pallas import tpu as pltpu

F32 = jnp.float32
BF16 = jnp.bfloat16

HEAD_DIM = 64
DECAY_LORA = 32
AAA_LORA = 32
GATE_LORA = 96
LORA_PAD = 256
DILATED = ((128, 1), (512, 4), (2048, 16))
WIN_MAX = 2048
ROPE_THETA = 10000.0
N_GROUPS = 4
EXPERTS_PER_GROUP = 8
N_EXPERTS = N_GROUPS * EXPERTS_PER_GROUP
PAST_LEN = 16384
RMS_EPS = 1e-6
LNX_EPS = 1e-5 * HEAD_DIM
NEG_INF = -1e30
LANES = 128
SUBLANES = 8
VMEM_LIMIT = 52 * 1024 * 1024
ATTN_UNROLL = 8


def _dot(a, b, precise):
    if precise:
        return _dot3(a, b)
    return jnp.dot(a.astype(BF16), b.astype(BF16), preferred_element_type=F32)


def _dot_nt(a, b, precise):
    if precise:
        return _dot3(a, b, nt=True)
    dn = (((1,), (1,)), ((), ()))
    return lax.dot_general(a.astype(BF16), b.astype(BF16), dn, preferred_element_type=F32)


def _split_bf16(a):
    hi = a.astype(BF16)
    return hi, (a - hi.astype(F32)).astype(BF16)


def _dot3(a, b, nt=False):
    dn = (((1,), (1 if nt else 0,)), ((), ()))
    mm = lambda x, y: lax.dot_general(x, y, dn, preferred_element_type=F32)
    a_hi, a_lo = _split_bf16(a)
    b_hi, b_lo = _split_bf16(b)
    return mm(a_hi, b_hi) + (mm(a_hi, b_lo) + mm(a_lo, b_hi))


def _div_pow2(x, n):
    assert n & (n - 1) == 0
    return lax.shift_right_arithmetic(x, jnp.int32(n.bit_length() - 1))


def _mod_pow2(x, n):
    assert n & (n - 1) == 0
    return jnp.bitwise_and(x, n - 1)


def _sigmoid(x):
    return 1.0 / (1.0 + jnp.exp(-x))


def _rms(x, g):
    return x * lax.rsqrt(jnp.mean(x * x, axis=-1, keepdims=True) + RMS_EPS) * g


def _inproj_kernel(x_ref, nw_ref, w_ref, mu_ref, sprev_ref, cos_ref, sin_ref,
                   w0_ref, w2_ref, a0_ref, a2_ref, g2_ref,
                   r_ref, k_ref, v_ref, dec_ref, a_ref, g_ref, q_ref, ka_ref, va_ref, plast_ref,
                   carry_ref, *, precise, seq_len, tiles_per_seq, cw, cp):
    tm = x_ref.shape[0]
    u = _rms(x_ref[...], nw_ref[...])
    proj = _dot(u, w_ref[...], precise)
    pc = proj[:, :cp]
    row = lax.broadcasted_iota(jnp.int32, (tm, 1), 0)
    prev = pltpu.roll(pc, 1, axis=0)
    if tiles_per_seq is None:
        prev = jnp.where(_mod_pow2(row, seq_len) == 0, sprev_ref[...], prev)
        plast_ref[...] = pc
    else:
        j = pl.program_id(0) % tiles_per_seq
        first = jnp.where(j == 0, sprev_ref[0], carry_ref[...])
        prev = jnp.where(row == 0, first, prev)
        carry_ref[...] = pc[tm - 1:tm, :]
        plast_ref[0] = pc[tm - 1:tm, :]
    pm = pc + (prev - pc) * mu_ref[...]
    r_ref[...] = pm[:, 0:cw]
    k_ref[...] = pm[:, cw:2 * cw]
    v_ref[...] = pm[:, 2 * cw:3 * cw]
    lora = pm[:, 3 * cw:cp]
    zw = w0_ref[...] + _dot(jnp.tanh(lora), w2_ref[...], precise)
    nz = -zw
    softplus = jnp.maximum(nz, 0.0) + jnp.log(1.0 + jnp.exp(-jnp.abs(nz)))
    dec_ref[...] = jnp.exp(-jnp.exp(-softplus - 0.5))
    a_ref[...] = _sigmoid(a0_ref[...] + _dot(lora, a2_ref[...], precise))
    g_ref[...] = _dot(_sigmoid(lora), g2_ref[...], precise)

    cos = jnp.concatenate([cos_ref[...]] * (cw // LANES), axis=1)
    sin = jnp.concatenate([sin_ref[...]] * (cw // LANES), axis=1)
    lane = lax.broadcasted_iota(jnp.int32, (tm, cw), 1)
    lo_half = (lane % HEAD_DIM) < (HEAD_DIM // 2)

    def rope(t):
        partner = jnp.where(lo_half, pltpu.roll(t, cw - HEAD_DIM // 2, axis=1),
                            pltpu.roll(t, HEAD_DIM // 2, axis=1))
        return t * cos + partner * sin

    q_ref[...] = rope(proj[:, cp:cp + cw])
    ka_ref[...] = rope(proj[:, cp + cw:cp + 2 * cw])
    va_ref[...] = proj[:, cp + 2 * cw:cp + 3 * cw]


def _inproj(x2d, sprev, pos_tab, wts, *, precise, seq_len, tm):
    n, d = x2d.shape
    cw, cp = wts["cw"], wts["cp"]
    flat = tm % seq_len == 0 and tm >= seq_len
    nt = n // tm
    tiles_per_seq = None if flat else seq_len // tm
    cos_tab, sin_tab = pos_tab
    ntab = cos_tab.shape[0] // tm
    wdt = F32 if precise else BF16

    const = lambda shape: pl.BlockSpec(shape, lambda i: (0,) * len(shape))
    tok = lambda c: pl.BlockSpec((tm, c), lambda i: (i, 0))
    if flat:
        sprev_spec = pl.BlockSpec((tm, cp), lambda i: (i, 0))
        plast_shape = jax.ShapeDtypeStruct((n, cp), F32)
        plast_spec = pl.BlockSpec((tm, cp), lambda i: (i, 0))
    else:
        sprev_spec = pl.BlockSpec((1, 1, cp), lambda i: (i // tiles_per_seq, 0, 0))
        plast_shape = jax.ShapeDtypeStruct((n // seq_len, 1, cp), F32)
        plast_spec = pl.BlockSpec((1, 1, cp), lambda i: (i // tiles_per_seq, 0, 0))
    in_specs = [
        tok(d), const((1, d)), const((d, cp + 3 * cw)), const((1, cp)), sprev_spec,
        pl.BlockSpec((tm, LANES), lambda i: (i % ntab, 0)),
        pl.BlockSpec((tm, LANES), lambda i: (i % ntab, 0)),
        const((1, cw)), const((LORA_PAD, cw)), const((1, cw)), const((LORA_PAD, cw)),
        const((LORA_PAD, cw)),
    ]
    out_shape = [jax.ShapeDtypeStruct((n, cw), F32)] * 9 + [plast_shape]
    out_specs = [tok(cw)] * 9 + [plast_spec]
    kern = functools.partial(_inproj_kernel, precise=precise, seq_len=seq_len,
                             tiles_per_seq=tiles_per_seq, cw=cw, cp=cp)
    return pl.pallas_call(
        kern, grid=(nt,), in_specs=in_specs, out_specs=out_specs, out_shape=out_shape,
        scratch_shapes=[pltpu.VMEM((1, cp), F32)],
        compiler_params=pltpu.CompilerParams(dimension_semantics=("arbitrary",),
                                             vmem_limit_bytes=VMEM_LIMIT),
        name="inproj_precise" if precise else "inproj",
    )(x2d, wts["norm_mix"], wts["w_cat"].astype(wdt), wts["mu_cat"], sprev, cos_tab, sin_tab,
      wts["w0"], wts["w2p"].astype(wdt), wts["a0"], wts["a2p"].astype(wdt), wts["g2p"].astype(wdt))


def _inproj_cm_kernel(x_ref, nw_ref, wt_ref, wa_ref, mu_ref, sprev_ref, cos_ref, sin_ref,
                      w0_ref, w2_ref, a0_ref, a2_ref, g2_ref,
                      r_ref, k_ref, v_ref, dec_ref, a_ref, g_ref, q_ref, ka_ref, va_ref, plast_ref,
                      carry_ref, *, tiles_per_seq, cw, cp):
    tm = x_ref.shape[0]
    nh = cw // HEAD_DIM
    u = _rms(x_ref[...], nw_ref[...]).astype(BF16)
    pt = _dot_nt(wt_ref[...], u, False)
    j = pl.program_id(0) % tiles_per_seq
    plast_ref[0] = pt[:, tm - LANES:]
    lanes_of = lambda z: jnp.concatenate([z] * (tm // LANES), axis=1)
    cm = lambda z: z.reshape(HEAD_DIM, nh, tm)
    r_ref[...] = cm(pt[0:cw])
    k_ref[...] = cm(pt[cw:2 * cw])
    for i in range(v_ref.shape[1]):
        v_ref[:, i] = cm(pt[2 * cw:3 * cw])
    pl_ = pt[3 * cw:cp]
    before = jnp.where(j == 0, sprev_ref[0], carry_ref[...])
    lane = lax.broadcasted_iota(jnp.int32, (cp - 3 * cw, LANES), 1)
    prev = pltpu.roll(pl_, 1, axis=1)
    head_blk = jnp.where(lane == 0, pltpu.roll(before, 1, axis=1), prev[:, :LANES])
    prev = jnp.concatenate([head_blk, prev[:, LANES:]], axis=1) if tm > LANES else head_blk
    carry_ref[...] = pl_[:, tm - LANES:]
    lora = pl_ + (prev - pl_) * lanes_of(mu_ref[...])
    zw = lanes_of(w0_ref[...]) + _dot(w2_ref[...], jnp.tanh(lora), False)
    nz = -zw
    softplus = jnp.maximum(nz, 0.0) + jnp.log(1.0 + jnp.exp(-jnp.abs(nz)))
    dec_ref[...] = cm(jnp.exp(-jnp.exp(-softplus - 0.5)))
    a_ref[...] = cm(_sigmoid(lanes_of(a0_ref[...]) + _dot(a2_ref[...], lora, False)))
    g_ref[...] = _dot(g2_ref[...], _sigmoid(lora), False).T

    proj = _dot(u, wa_ref[...], False)
    cos = jnp.concatenate([cos_ref[...]] * (cw // LANES), axis=1)
    sin = jnp.concatenate([sin_ref[...]] * (cw // LANES), axis=1)
    lane_c = lax.broadcasted_iota(jnp.int32, (tm, cw), 1)
    lo_half = _mod_pow2(lane_c, HEAD_DIM) < (HEAD_DIM // 2)

    def rope(t):
        partner = jnp.where(lo_half, pltpu.roll(t, cw - HEAD_DIM // 2, axis=1),
                            pltpu.roll(t, HEAD_DIM // 2, axis=1))
        return t * cos + partner * sin

    q_ref[...] = rope(proj[:, 0:cw])
    ka_ref[...] = rope(proj[:, cw:2 * cw])
    va_ref[...] = proj[:, 2 * cw:3 * cw]


def _inproj_cm(x2d, pos_tab, wts, *, b, seq_len, tm):
    n, d = x2d.shape
    cw, cp = wts["cw"], wts["cp"]
    nh = cw // HEAD_DIM
    lp = cp - 3 * cw
    tiles_per_seq = seq_len // tm
    cos_tab, sin_tab = pos_tab
    const = lambda shape: pl.BlockSpec(shape, lambda i: (0,) * len(shape))
    tok = lambda c: pl.BlockSpec((tm, c), lambda i: (i, 0))
    chan = pl.BlockSpec((HEAD_DIM, nh, tm), lambda i: (0, i // tiles_per_seq, i % tiles_per_seq))
    ndup = max(1, LANES // (b * nh))
    chan2 = pl.BlockSpec((HEAD_DIM, ndup, nh, tm), lambda i: (0, 0, i // tiles_per_seq, i % tiles_per_seq))
    per_b = lambda rows: pl.BlockSpec((1, rows, LANES), lambda i: (i // tiles_per_seq, 0, 0))
    in_specs = [
        tok(d), const((1, d)), const((cp, d)), const((d, 3 * cw)), const((lp, LANES)), per_b(lp),
        pl.BlockSpec((tm, LANES), lambda i: (i % tiles_per_seq, 0)),
        pl.BlockSpec((tm, LANES), lambda i: (i % tiles_per_seq, 0)),
        const((cw, LANES)), const((cw, LORA_PAD)), const((cw, LANES)), const((cw, LORA_PAD)),
        const((cw, LORA_PAD)),
    ]
    chan_shape = jax.ShapeDtypeStruct((HEAD_DIM, b * nh, seq_len), F32)
    chan2_shape = jax.ShapeDtypeStruct((HEAD_DIM, ndup, b * nh, seq_len), F32)
    out_shape = [chan_shape, chan_shape, chan2_shape, chan_shape, chan_shape] + [
        jax.ShapeDtypeStruct((n, cw), F32)] * 4 + [jax.ShapeDtypeStruct((b, cp, LANES), F32)]
    out_specs = [chan, chan, chan2, chan, chan] + [tok(cw)] * 4 + [per_b(cp)]
    perm = lambda z: jnp.transpose(z.reshape(z.shape[0], -1, nh, HEAD_DIM), (0, 1, 3, 2)).reshape(z.shape)
    rp = wts["rp"]
    w = wts["w_cat"]
    wt = jnp.concatenate([perm(w[:, :3 * cw]), w[:, 3 * cw:cp]], axis=1).T.astype(BF16)
    col = lambda z: jnp.broadcast_to(z.reshape(-1, 1), (z.size, LANES))
    zero_shift = jnp.zeros((b, lp, LANES), F32)
    outs = pl.pallas_call(
        functools.partial(_inproj_cm_kernel, tiles_per_seq=tiles_per_seq, cw=cw, cp=cp),
        grid=(n // tm,), in_specs=in_specs, out_specs=out_specs, out_shape=out_shape,
        scratch_shapes=[pltpu.VMEM((lp, LANES), F32)],
        compiler_params=pltpu.CompilerParams(dimension_semantics=("arbitrary",),
                                             vmem_limit_bytes=VMEM_LIMIT),
        name="inproj",
    )(x2d, wts["norm_mix"], wt, w[:, cp:].astype(BF16), col(wts["mu_cat"][:, 3 * cw:]), zero_shift,
      cos_tab, sin_tab,
      col(perm(wts["w0"])), perm(wts["w2p"]).T.astype(BF16), col(perm(wts["a0"])),
      perm(wts["a2p"]).T.astype(BF16), wts["g2p"].T.astype(BF16))
    last = outs[9][:, :, LANES - 1]
    unperm = lambda z: jnp.transpose(z.reshape(z.shape[0], -1, HEAD_DIM, nh), (0, 1, 3, 2)).reshape(z.shape)
    plast = jnp.concatenate([unperm(last[:, :3 * cw]), last[:, 3 * cw:rp]], axis=1)
    return list(outs[:9]) + [plast]


def _lane_fold(x, kpar):
    s = LANES // 2
    while s >= LANES // kpar:
        x = x + pltpu.roll(x, s, axis=x.ndim - 1)
        s //= 2
    return x


def _wkv_kernel(*refs, kpar, shift):
    if shift:
        (r_ref, k_ref, w_ref, a_ref, v_ref, mur_ref, muk_ref, muv_ref, p0r_ref, p0k_ref, p0v_ref,
         kk_ref, ka_ref, rk_ref, lnw_ref, lnb_ref, s0_ref,
         y_ref, s_ref, aop_ref, wr_ref, bop_ref, km_ref, sc_ref, vs_ref, cr_ref, ck_ref, cv_ref) = refs
    else:
        (r_ref, k_ref, w_ref, a_ref, v_ref, kk_ref, ka_ref, rk_ref, lnw_ref, lnb_ref, s0_ref,
         y_ref, s_ref, aop_ref, wr_ref, bop_ref, km_ref, sc_ref, vs_ref) = refs
    tt, nr, _ = r_ref.shape
    nv = v_ref.shape[1]
    nvb = nv // SUBLANES

    @pl.when(pl.program_id(1) == 0)
    def _():
        s_ref[...] = s0_ref[...]
        if shift:
            cr_ref[...] = p0r_ref[...]
            ck_ref[...] = p0k_ref[...]
            cv_ref[...] = p0v_ref[...]

    def shifted(x_ref, c_ref, mu_ref):
        raw = x_ref[...]
        prev = jnp.concatenate([c_ref[...][None], raw[:tt - 1]], axis=0) if tt > 1 else c_ref[...][None]
        c_ref[...] = raw[tt - 1]
        return raw + (prev - raw) * mu_ref[...]

    if shift:
        r = shifted(r_ref, cr_ref, mur_ref)
        k = shifted(k_ref, ck_ref, muk_ref)
        vs_ref[...] = shifted(v_ref, cv_ref, muv_ref)
    else:
        r = r_ref[...]
        k = k_ref[...]
        vs_ref[...] = v_ref[...]
    a = a_ref[...]
    kkr = k * kk_ref[...]
    ss = _lane_fold(jnp.sum(kkr * kkr, axis=1, keepdims=True), kpar)
    kkn = kkr / jnp.maximum(jnp.sqrt(ss), 1e-12)
    bop = kkn * a
    km = k * (1.0 + (a - 1.0) * ka_ref[...])
    aop_ref[...] = -kkn
    bop_ref[...] = bop
    km_ref[...] = km
    wr_ref[...] = w_ref[...] * r
    br = _lane_fold(jnp.sum(bop * r, axis=1, keepdims=True), kpar)
    kr = _lane_fold(jnp.sum(km * r, axis=1, keepdims=True), kpar)
    bonus = _lane_fold(jnp.sum(r * km * rk_ref[...], axis=1, keepdims=True), kpar)
    sc_ref[:, 0:1, :] = br
    sc_ref[:, 1:2, :] = kr

    def step(t, carry):
        vv = [vs_ref[t, vb * SUBLANES:(vb + 1) * SUBLANES, :] for vb in range(nvb)]
        acc_sa = [None] * nvb
        acc_y = [None] * nvb
        for kr_i in range(nr):
            a_row = aop_ref[t, kr_i:kr_i + 1, :]
            wr_row = wr_ref[t, kr_i:kr_i + 1, :]
            for vb in range(nvb):
                blk = s_ref[kr_i, vb * SUBLANES:(vb + 1) * SUBLANES, :]
                pa = blk * a_row
                py = blk * wr_row
                acc_sa[vb] = pa if acc_sa[vb] is None else acc_sa[vb] + pa
                acc_y[vb] = py if acc_y[vb] is None else acc_y[vb] + py
        sa = [_lane_fold(x, kpar) for x in acc_sa]
        yp = [_lane_fold(x, kpar) for x in acc_y]
        for kr_i in range(nr):
            w_row = w_ref[t, kr_i:kr_i + 1, :]
            b_row = bop_ref[t, kr_i:kr_i + 1, :]
            k_row = km_ref[t, kr_i:kr_i + 1, :]
            for vb in range(nvb):
                sl = slice(vb * SUBLANES, (vb + 1) * SUBLANES)
                s_ref[kr_i, sl, :] = s_ref[kr_i, sl, :] * w_row + sa[vb] * b_row + vv[vb] * k_row
        br_row = sc_ref[t, 0:1, :]
        kr_row = sc_ref[t, 1:2, :]
        for vb in range(nvb):
            y_ref[t, vb * SUBLANES:(vb + 1) * SUBLANES, :] = yp[vb] + sa[vb] * br_row + vv[vb] * kr_row
        return carry

    lax.fori_loop(0, tt, step, 0)

    y = y_ref[...]
    mean = jnp.mean(y, axis=1, keepdims=True)
    yc = y - mean
    var = jnp.mean(yc * yc, axis=1, keepdims=True)
    y_ref[...] = yc * lax.rsqrt(var + LNX_EPS) * lnw_ref[...] + lnb_ref[...] + bonus * vs_ref[...]


def _to_lane_layout(x, b, t, h):
    return jnp.transpose(x.reshape(b, t, h, HEAD_DIM), (1, 3, 0, 2)).reshape(t, HEAD_DIM, b * h)


def _wkv(r, k, v, w, a, state, prm, *, b, t, h, tt, shift=None):
    p = b * h
    if p >= LANES:
        assert p % LANES == 0
        g, kpar = p // LANES, 1
    else:
        assert LANES % p == 0
        g, kpar = 1, LANES // p
    nr = HEAD_DIM // kpar

    def krows(x):
        if kpar > 1:
            return x.reshape(1, x.shape[0], nr, LANES)
        return jnp.transpose(x.reshape(x.shape[0], HEAD_DIM, g, LANES), (2, 0, 1, 3))

    def vrows(x):
        if kpar > 1:
            return x[None] if x.shape[-1] == LANES else jnp.concatenate([x] * kpar, axis=-1)[None]
        return jnp.transpose(x.reshape(x.shape[0], HEAD_DIM, g, LANES), (2, 0, 1, 3))

    rl, kl, wl, al = (krows(z) for z in (r, k, w, a))
    vl = vrows(v)

    def prm_k(z):
        x = jnp.broadcast_to(z.T[:, None, :], (HEAD_DIM, b, h)).reshape(1, HEAD_DIM, p)
        return krows(x)[:, 0]

    def prm_v(z):
        x = jnp.broadcast_to(z.T[:, None, :], (HEAD_DIM, b, h)).reshape(1, HEAD_DIM, p)
        return vrows(x)[:, 0]

    st = jnp.transpose(state.reshape(p, HEAD_DIM, HEAD_DIM), (2, 1, 0))
    if kpar > 1:
        st = jnp.transpose(st.reshape(nr, kpar, HEAD_DIM, p), (0, 2, 1, 3)).reshape(1, nr, HEAD_DIM, LANES)
    else:
        st = jnp.transpose(st.reshape(HEAD_DIM, HEAD_DIM, g, LANES), (2, 0, 1, 3))

    nt = t // tt
    tile_k = pl.BlockSpec((None, tt, nr, LANES), lambda gi, ti: (gi, ti, 0, 0))
    tile_v = pl.BlockSpec((None, tt, HEAD_DIM, LANES), lambda gi, ti: (gi, ti, 0, 0))
    par_k = pl.BlockSpec((None, nr, LANES), lambda gi, ti: (gi, 0, 0))
    par_v = pl.BlockSpec((None, HEAD_DIM, LANES), lambda gi, ti: (gi, 0, 0))
    st_spec = pl.BlockSpec((None, nr, HEAD_DIM, LANES), lambda gi, ti: (gi, 0, 0, 0))
    shift_args, shift_specs, shift_scratch = [], [], []
    if shift is not None:
        mu_r, mu_k, mu_v, p0_r, p0_k, p0_v = shift
        lane0 = lambda z: _to_lane_layout(z, b, 1, h)
        shift_args = [prm_k(mu_r), prm_k(mu_k), prm_v(mu_v),
                      krows(lane0(p0_r))[:, 0], krows(lane0(p0_k))[:, 0], vrows(lane0(p0_v))[:, 0]]
        shift_specs = [par_k, par_k, par_v, par_k, par_k, par_v]
        shift_scratch = [pltpu.VMEM((nr, LANES), F32)] * 2 + [pltpu.VMEM((HEAD_DIM, LANES), F32)]
    y, s_fin = pl.pallas_call(
        functools.partial(_wkv_kernel, kpar=kpar, shift=shift is not None),
        grid=(g, nt),
        in_specs=[tile_k, tile_k, tile_k, tile_k, tile_v] + shift_specs
                 + [par_k, par_k, par_k, par_v, par_v, st_spec],
        out_specs=[tile_v, st_spec],
        out_shape=[jax.ShapeDtypeStruct((g, t, HEAD_DIM, LANES), F32),
                   jax.ShapeDtypeStruct((g, nr, HEAD_DIM, LANES), F32)],
        scratch_shapes=[pltpu.VMEM((tt, nr, LANES), F32)] * 4 + [pltpu.VMEM((tt, SUBLANES, LANES), F32),
                                                                 pltpu.VMEM((tt, HEAD_DIM, LANES), F32)]
                       + shift_scratch,
        compiler_params=pltpu.CompilerParams(dimension_semantics=("arbitrary", "arbitrary"),
                                             vmem_limit_bytes=VMEM_LIMIT),
        name="wkv",
    )(rl, kl, wl, al, vl, *shift_args, prm_k(prm["k_k"]), prm_k(prm["k_a"]), prm_k(prm["r_k"]),
      prm_v(prm["lnx_w"]), prm_v(prm["lnx_b"]), st)

    if kpar > 1:
        yt = y[0, :, :, :p]
        sf = jnp.transpose(s_fin[0].reshape(nr, HEAD_DIM, kpar, p), (3, 1, 0, 2))
    else:
        yt = jnp.transpose(y, (1, 2, 0, 3)).reshape(t, HEAD_DIM, p)
        sf = jnp.transpose(s_fin, (0, 3, 2, 1)).reshape(p, HEAD_DIM, HEAD_DIM)
    y_tok = jnp.transpose(yt.reshape(t, HEAD_DIM, b, h), (2, 0, 3, 1)).reshape(b * t, h * HEAD_DIM)
    return y_tok, sf.reshape(b, h, HEAD_DIM, HEAD_DIM)


def _attn_prompt_kernel(q_ref, k_ref, v_ref, o_ref, ob_ref, lb_ref, qs_ref, vs_ref, kt_ref,
                        bias2_ref, bias1_ref, *, unroll):
    s_len = q_ref.shape[0]
    blk = 128
    nchunk = s_len // blk
    lane = lax.broadcasted_iota(jnp.int32, (blk, LANES), 1)
    head0 = lane < HEAD_DIM
    head1 = jnp.logical_not(head0)
    qi = _mod_pow2(lax.broadcasted_iota(jnp.int32, (2 * blk, 2 * blk), 0), blk)
    kj = lax.broadcasted_iota(jnp.int32, (2 * blk, 2 * blk), 1)
    bias2_ref[...] = jnp.where(kj < blk, jnp.where(kj >= qi, 0.0, NEG_INF),
                               jnp.where(kj - blk <= qi, 0.0, NEG_INF))
    qi1 = _mod_pow2(lax.broadcasted_iota(jnp.int32, (2 * blk, blk), 0), blk)
    kj1 = lax.broadcasted_iota(jnp.int32, (2 * blk, blk), 1)
    bias1_ref[...] = jnp.where(kj1 <= qi1, 0.0, NEG_INF)
    scale = 1.0 / math.sqrt(HEAD_DIM)

    def finish(bi, rows, s, v0, v1):
        s0, s1 = s[:blk], s[blk:]
        m0 = jnp.max(s0, axis=1, keepdims=True)
        m1 = jnp.max(s1, axis=1, keepdims=True)
        a0 = jnp.dot(jnp.exp(s0 - m0).astype(BF16), v0, preferred_element_type=F32)
        a1 = jnp.dot(jnp.exp(s1 - m1).astype(BF16), v1, preferred_element_type=F32)
        num = jnp.where(head0, a0, a1)
        den = pltpu.roll(jnp.where(head0, a1, a0), HEAD_DIM, axis=1)
        ob_ref[bi, rows, :] = num / den
        lb_ref[bi, rows, :] = jnp.where(head0, m0, m1) + jnp.log(den)

    for bi, (window, dil) in enumerate(DILATED):
        assert window // dil == blk
        unit = blk * dil
        nblk = s_len // unit

        def pos_rows(j, dil=dil, unit=unit, nblk=nblk):
            if dil == 1:
                return pl.ds(pl.multiple_of(j * blk, blk), blk)
            return pl.ds((j % nblk) * unit + j // nblk, blk, stride=dil)

        def stage(j, carry, pos_rows=pos_rows):
            rows = pos_rows(j)
            dst = pl.ds(pl.multiple_of(j * blk, blk), blk)
            q = q_ref[rows, :] * scale
            v = v_ref[rows, :]
            qs_ref[0, dst, :] = jnp.where(head0, q, 0.0).astype(BF16)
            qs_ref[1, dst, :] = jnp.where(head1, q, 0.0).astype(BF16)
            vs_ref[0, dst, :] = jnp.where(head0, v, 1.0).astype(BF16)
            vs_ref[1, dst, :] = jnp.where(head1, v, 1.0).astype(BF16)
            kt_ref[:, dst] = k_ref[rows, :].T.astype(BF16)
            return carry

        def first(c, carry, bi=bi, nblk=nblk, pos_rows=pos_rows):
            j = c * nblk
            src = pl.ds(pl.multiple_of(j * blk, blk), blk)
            qs = jnp.concatenate([qs_ref[0, src, :], qs_ref[1, src, :]], axis=0)
            s = jnp.dot(qs, kt_ref[:, src], preferred_element_type=F32) + bias1_ref[...]
            finish(bi, pos_rows(j), s, vs_ref[0, src, :], vs_ref[1, src, :])
            return carry

        def later(idx, carry, bi=bi, nblk=nblk, pos_rows=pos_rows):
            j = (idx // (nblk - 1)) * nblk + 1 + idx % (nblk - 1)
            src = pl.ds(pl.multiple_of(j * blk, blk), blk)
            keys = pl.ds(pl.multiple_of((j - 1) * blk, blk), 2 * blk)
            qs = jnp.concatenate([qs_ref[0, src, :], qs_ref[1, src, :]], axis=0)
            s = jnp.dot(qs, kt_ref[:, keys], preferred_element_type=F32) + bias2_ref[...]
            finish(bi, pos_rows(j), s, vs_ref[0, keys, :], vs_ref[1, keys, :])
            return carry

        lax.fori_loop(0, nchunk, stage, 0, unroll=4)
        lax.fori_loop(0, dil, first, 0, unroll=min(dil, unroll))
        if nblk > 1:
            lax.fori_loop(0, dil * (nblk - 1), later, 0, unroll=unroll)

    def merge(ci, carry):
        sl = pl.ds(pl.multiple_of(ci * blk, blk), blk)
        l0, l1, l2 = lb_ref[0, sl, :], lb_ref[1, sl, :], lb_ref[2, sl, :]
        m = jnp.maximum(jnp.maximum(l0, l1), l2)
        w0, w1, w2 = jnp.exp(l0 - m), jnp.exp(l1 - m), jnp.exp(l2 - m)
        tot = w0 + w1 + w2
        o_ref[sl, :] = (w0 * ob_ref[0, sl, :] + w1 * ob_ref[1, sl, :] + w2 * ob_ref[2, sl, :]) / tot
        return carry

    lax.fori_loop(0, s_len // blk, merge, 0)


def _attn_prompt(q, k, v, *, b, s):
    cw = q.shape[1]
    hp = cw // LANES
    assert s % (128 * DILATED[-1][1]) == 0
    spec = pl.BlockSpec((None, s, LANES), lambda bi, hi: (bi, 0, hi))
    out = pl.pallas_call(
        functools.partial(_attn_prompt_kernel, unroll=ATTN_UNROLL), grid=(b, hp),
        in_specs=[spec, spec, spec], out_specs=spec,
        out_shape=jax.ShapeDtypeStruct((b, s, cw), F32),
        scratch_shapes=[pltpu.VMEM((3, s, LANES), F32), pltpu.VMEM((3, s, LANES), F32),
                        pltpu.VMEM((2, s, LANES), BF16), pltpu.VMEM((2, s, LANES), BF16),
                        pltpu.VMEM((LANES, s), BF16),
                        pltpu.VMEM((256, 256), F32), pltpu.VMEM((256, 128), F32)],
        compiler_params=pltpu.CompilerParams(dimension_semantics=("arbitrary", "arbitrary"),
                                             vmem_limit_bytes=VMEM_LIMIT),
        name="attn_prompt",
    )(q.reshape(b, s, cw), k.reshape(b, s, cw), v.reshape(b, s, cw))
    return out.reshape(b * s, cw)


def _attn_decode_kernel(q_ref, kn_ref, vn_ref, kc_ref, vc_ref, o_ref, *, t_new):
    wc, cw = kc_ref.shape
    nh = cw // HEAD_DIM
    npad = kn_ref.shape[0]
    rows = nh * t_new
    q = q_ref[...]
    qt = jnp.concatenate([q] * nh, axis=0)
    rh = _div_pow2(lax.broadcasted_iota(jnp.int32, (rows, cw), 0), t_new)
    lh = _div_pow2(lax.broadcasted_iota(jnp.int32, (rows, cw), 1), HEAD_DIM)
    own_head = rh == lh
    qbd = jnp.where(own_head, qt, 0.0)
    scale = 1.0 / math.sqrt(HEAD_DIM)
    s_c = _dot3(qbd, kc_ref[...], nt=True) * scale
    s_n = _dot3(qbd, kn_ref[...], nt=True) * scale
    t_c = _mod_pow2(lax.broadcasted_iota(jnp.int32, (rows, wc), 0), t_new)
    dist_c = wc + t_c - lax.broadcasted_iota(jnp.int32, (rows, wc), 1)
    t_n = _mod_pow2(lax.broadcasted_iota(jnp.int32, (rows, npad), 0), t_new)
    j_n = lax.broadcasted_iota(jnp.int32, (rows, npad), 1)
    dist_n = t_n - j_n
    ecs, ens, ls, lses = [], [], [], []
    for window, dil in DILATED:
        ok_c = jnp.logical_and(_mod_pow2(dist_c, dil) == 0, dist_c <= window)
        ok_n = jnp.logical_and(jnp.logical_and(dist_n >= 0, j_n < t_new),
                               jnp.logical_and(_mod_pow2(dist_n, dil) == 0, dist_n <= window))
        sc = jnp.where(ok_c, s_c, NEG_INF)
        sn = jnp.where(ok_n, s_n, NEG_INF)
        m = jnp.maximum(jnp.max(sc, axis=1, keepdims=True), jnp.max(sn, axis=1, keepdims=True))
        ec = jnp.exp(sc - m)
        en = jnp.exp(sn - m)
        l = jnp.sum(ec, axis=1, keepdims=True) + jnp.sum(en, axis=1, keepdims=True)
        ecs.append(ec)
        ens.append(en)
        ls.append(l)
        lses.append(m + jnp.log(l))
    o_all = (_dot3(jnp.concatenate(ecs, axis=0), vc_ref[...])
             + _dot3(jnp.concatenate(ens, axis=0), vn_ref[...]))
    outs = [o_all[i * rows:(i + 1) * rows] / ls[i] for i in range(len(DILATED))]
    m = jnp.maximum(jnp.maximum(lses[0], lses[1]), lses[2])
    ws = [jnp.exp(x - m) for x in lses]
    full = (ws[0] * outs[0] + ws[1] * outs[1] + ws[2] * outs[2]) / (ws[0] + ws[1] + ws[2])
    full = jnp.where(own_head, full, 0.0).reshape(nh, t_new, cw)
    o_ref[...] = jnp.sum(full, axis=0)


def _attn_decode(q, k, v, cache_k, cache_v, *, b, t):
    cw = q.shape[1]
    wc = cache_k.shape[1]
    npad = LANES
    pad = lambda z: jnp.pad(z.reshape(b, t, cw), ((0, 0), (0, npad - t), (0, 0)))
    new_spec = pl.BlockSpec((None, npad, cw), lambda bi: (bi, 0, 0))
    cache_spec = pl.BlockSpec((None, wc, cw), lambda bi: (bi, 0, 0))
    tok_spec = pl.BlockSpec((None, t, cw), lambda bi: (bi, 0, 0))
    out = pl.pallas_call(
        functools.partial(_attn_decode_kernel, t_new=t), grid=(b,),
        in_specs=[tok_spec, new_spec, new_spec, cache_spec, cache_spec], out_specs=tok_spec,
        out_shape=jax.ShapeDtypeStruct((b, t, cw), F32),
        compiler_params=pltpu.CompilerParams(dimension_semantics=("arbitrary",),
                                             vmem_limit_bytes=VMEM_LIMIT),
        name="attn_decode",
    )(q.reshape(b, t, cw), pad(k), pad(v), cache_k, cache_v)
    return out.reshape(b * t, cw)


def _route(logits):
    lane_i = lax.broadcasted_iota(jnp.int32, logits.shape, 1)
    lane = lane_i.astype(F32)
    is_g = lane_i < N_GROUPS
    lg = jnp.where(is_g, logits, NEG_INF)
    mg = jnp.max(lg, axis=1, keepdims=True)
    pg_sel = 1.0 / jnp.sum(jnp.where(is_g, jnp.exp(lg - mg), 0.0), axis=1, keepdims=True)
    gsel = jnp.min(jnp.where(lg == mg, lane, float(LANES)), axis=1, keepdims=True)
    e_lane = lane_i - N_GROUPS
    e_group = _div_pow2(e_lane, EXPERTS_PER_GROUP).astype(F32)
    in_sel = jnp.logical_and(jnp.logical_and(e_lane >= 0, e_lane < N_EXPERTS), e_group == gsel)
    le = jnp.where(in_sel, logits, NEG_INF)
    me = jnp.max(le, axis=1, keepdims=True)
    ee = jnp.where(in_sel, jnp.exp(le - me), 0.0)
    pe = jnp.where(in_sel, ee / jnp.sum(ee, axis=1, keepdims=True), -1.0)
    p1 = jnp.max(pe, axis=1, keepdims=True)
    i1 = jnp.min(jnp.where(pe == p1, lane, float(LANES)), axis=1, keepdims=True)
    pe2 = jnp.where(lane == i1, -1.0, pe)
    p2 = jnp.max(pe2, axis=1, keepdims=True)
    i2 = jnp.min(jnp.where(pe2 == p2, lane, float(LANES)), axis=1, keepdims=True)
    tot = p1 + p2
    return jnp.where(lane == i1, pg_sel * p1 / tot, 0.0) + jnp.where(lane == i2, pg_sel * p2 / tot, 0.0)


def _post_kernel(x_ref, rw_ref, g_ref, att_ref, p_ref, an_ref, wo_ref, nf_ref, wr_ref, br_ref,
                 wg_ref, wu_ref, wd_ref, np_ref, wpg_ref, wpp_ref, nfin_ref,
                 y_ref, h1_ref, u2_ref, gate_ref, acc_ref, *, precise):
    e = pl.program_id(1)
    cw = rw_ref.shape[1]
    ne_step, ff, _ = wd_ref.shape

    @pl.when(e == 0)
    def _():
        mix_r = rw_ref[...] * g_ref[...]
        mix_a = _rms(att_ref[...], an_ref[...])
        h1 = x_ref[...] + _dot(mix_r, wo_ref[0:cw, :], precise) + _dot(mix_a, wo_ref[cw:2 * cw, :], precise)
        h1_ref[...] = h1
        u2 = _rms(h1, nf_ref[...])
        u2_ref[...] = u2.astype(u2_ref.dtype)
        gate_ref[...] = _route(_dot(u2, wr_ref[...], True) + br_ref[...])
        acc_ref[...] = jnp.zeros_like(acc_ref)

    lane = lax.broadcasted_iota(jnp.int32, gate_ref.shape, 1)
    u2 = u2_ref[...]
    gate = gate_ref[...]
    hidden = []
    for i in range(ne_step):
        ge = jnp.sum(jnp.where(lane == e * ne_step + i + N_GROUPS, gate, 0.0), axis=1, keepdims=True)
        gp = _dot(u2, wg_ref[i], precise)
        up = _dot(u2, wu_ref[i], precise)
        hidden.append((gp * _sigmoid(gp) * up * ge).astype(u2.dtype))
    acc_ref[...] += _dot(jnp.concatenate(hidden, axis=1), wd_ref[...].reshape(ne_step * ff, -1), precise)

    @pl.when(e == pl.num_programs(1) - 1)
    def _():
        h2 = h1_ref[...] + acc_ref[...]
        u3 = _rms(h2, np_ref[...])
        h3 = h2 + _sigmoid(_dot(u3, wpg_ref[...], precise)) * _dot(p_ref[...], wpp_ref[...], precise)
        y_ref[...] = _rms(h3, nfin_ref[...])


def _post(x2d, rw, g, att, p2d, wts, *, precise, tm, ne_step):
    n, d = x2d.shape
    cw = rw.shape[1]
    pd = p2d.shape[1]
    ne, _, ff = wts["w_g"].shape
    assert ne % ne_step == 0
    wdt = F32 if precise else BF16
    tok = lambda c: pl.BlockSpec((tm, c), lambda i, e: (i, 0))
    const = lambda shape: pl.BlockSpec(shape, lambda i, e: (0,) * len(shape))
    in_specs = [
        tok(d), tok(cw), tok(cw), tok(cw), tok(pd),
        const((1, cw)), const((2 * cw, d)), const((1, d)), const((d, LANES)), const((1, LANES)),
        pl.BlockSpec((ne_step, d, ff), lambda i, e: (e, 0, 0)),
        pl.BlockSpec((ne_step, d, ff), lambda i, e: (e, 0, 0)),
        pl.BlockSpec((ne_step, ff, d), lambda i, e: (e, 0, 0)),
        const((1, d)), const((d, d)), const((pd, d)), const((1, d)),
    ]
    return pl.pallas_call(
        functools.partial(_post_kernel, precise=precise),
        grid=(n // tm, ne // ne_step), in_specs=in_specs, out_specs=tok(d),
        out_shape=jax.ShapeDtypeStruct((n, d), F32),
        scratch_shapes=[pltpu.VMEM((tm, d), F32), pltpu.VMEM((tm, d), wdt),
                        pltpu.VMEM((tm, LANES), F32), pltpu.VMEM((tm, d), F32)],
        compiler_params=pltpu.CompilerParams(dimension_semantics=("arbitrary", "arbitrary"),
                                             vmem_limit_bytes=VMEM_LIMIT),
        name="post_precise" if precise else "post",
    )(x2d, rw, g, att, p2d, wts["attn_out_norm"], wts["w_out"].astype(wdt), wts["norm_ffn"],
      wts["w_router"], wts["b_router"], wts["w_g"].astype(wdt), wts["w_u"].astype(wdt),
      wts["w_d"].astype(wdt),
      wts["norm_ple"], wts["ple_gate_w"].astype(wdt), wts["ple_proj"].astype(wdt), wts["norm_final"])


def _rope_tables(pos):
    half = HEAD_DIM // 2
    inv = np.power(ROPE_THETA, -np.arange(half, dtype=np.float64) * 2.0 / HEAD_DIM)
    ang = np.asarray(pos, np.float64)[:, None] * inv[None, :]
    cos, sin = np.cos(ang), np.sin(ang)
    cos_h = np.concatenate([cos, cos], axis=1)
    sin_h = np.concatenate([-sin, sin], axis=1)
    reps = LANES // HEAD_DIM
    return (jnp.asarray(np.concatenate([cos_h] * reps, axis=1), F32),
            jnp.asarray(np.concatenate([sin_h] * reps, axis=1), F32))


def _layer_weights(i, norm_mix, w_in, mu_shift, decay_w0, decay_w2, iclr_a0, iclr_a2, gate_g2,
                   k_k, k_a, r_k, lnx_w, lnx_b, attn_out_norm, w_out, norm_ffn,
                   router_group_w, router_group_b, router_expert_w, router_expert_b,
                   expert_w_gate, expert_w_up, expert_w_down, norm_ple, ple_gate_w, ple_proj, norm_final):
    d = w_in.shape[1]
    cw = decay_w0.shape[1]
    n_lora = DECAY_LORA + AAA_LORA + GATE_LORA
    rp = 3 * cw + n_lora
    cp = 3 * cw + LORA_PAD
    padc = lambda z: jnp.pad(z, ((0, 0), (0, LORA_PAD - n_lora)))
    w = w_in[i]
    w_cat = jnp.concatenate([padc(w[:, :rp]), w[:, rp:]], axis=1)
    padr = lambda z, lo: jnp.pad(z, ((lo, LORA_PAD - lo - z.shape[0]), (0, 0)))
    n_route = N_GROUPS + N_EXPERTS
    heads = cw // HEAD_DIM
    return dict(
        cw=cw, cp=cp, rp=rp,
        norm_mix=norm_mix[i][None], w_cat=w_cat, mu_cat=padc(mu_shift[i][None]),
        w0=decay_w0[i][None], w2p=padr(decay_w2[i], 0), a0=iclr_a0[i][None],
        a2p=padr(iclr_a2[i], DECAY_LORA), g2p=padr(gate_g2[i], DECAY_LORA + AAA_LORA),
        k_k=k_k[i].reshape(heads, HEAD_DIM), k_a=k_a[i].reshape(heads, HEAD_DIM), r_k=r_k[i],
        lnx_w=lnx_w[i].reshape(heads, HEAD_DIM), lnx_b=lnx_b[i].reshape(heads, HEAD_DIM),
        attn_out_norm=attn_out_norm[i][None], w_out=w_out[i], norm_ffn=norm_ffn[i][None],
        w_router=jnp.pad(jnp.concatenate([router_group_w[i], router_expert_w[i]], axis=1),
                         ((0, 0), (0, LANES - n_route))),
        b_router=jnp.pad(jnp.concatenate([router_group_b[i], router_expert_b[i]])[None],
                         ((0, 0), (0, LANES - n_route))),
        w_g=expert_w_gate[i], w_u=expert_w_up[i], w_d=expert_w_down[i],
        norm_ple=norm_ple[i][None], ple_gate_w=ple_gate_w[i], ple_proj=ple_proj[i],
        norm_final=norm_final[None],
    )


def _unpad_shift(pl_rows, wts):
    return pl_rows[:, :wts["rp"]]


def kernel(x_prompt, x_sample, cache_k_win, cache_v_win, state_wkv, state_shift, p_prompt, p_sample, norm_mix, w_in, mu_shift, decay_w0, decay_w2, iclr_a0, iclr_a2, gate_g2, k_k, k_a, r_k, lnx_w, lnx_b, attn_out_norm, w_out, norm_ffn, router_group_w, router_group_b, router_expert_w, router_expert_b, expert_w_gate, expert_w_up, expert_w_down, norm_ple, ple_gate_w, ple_proj, norm_final):
    b, s, d = x_prompt.shape
    db, t, _ = x_sample.shape
    depth = w_in.shape[0]
    assert depth == 1, "a deeper stack would chain the layer below over h"
    wts = _layer_weights(0, norm_mix, w_in, mu_shift, decay_w0, decay_w2, iclr_a0, iclr_a2, gate_g2,
                         k_k, k_a, r_k, lnx_w, lnx_b, attn_out_norm, w_out, norm_ffn,
                         router_group_w, router_group_b, router_expert_w, router_expert_b,
                         expert_w_gate, expert_w_up, expert_w_down, norm_ple, ple_gate_w, ple_proj,
                         norm_final)
    cw, cp, rp = wts["cw"], wts["cp"], wts["rp"]
    heads = cw // HEAD_DIM
    keep = min(WIN_MAX, s)
    padc = lambda z: jnp.pad(z, ((0, 0), (0, cp - rp)))

    tm_p = min(512, s)
    pos_p = _rope_tables(np.arange(s))
    r, k, v, w, a, g, q, ka, va, plast = _inproj_cm(
        x_prompt.reshape(b * s, d), pos_p, wts, b=b, seq_len=s, tm=tm_p)
    zero_wkv = jnp.zeros((b, heads, HEAD_DIM, HEAD_DIM), F32)
    lane_p = lambda z: z.reshape(-1, s).T.reshape(s, HEAD_DIM, -1)
    mu = wts["mu_cat"][0, :3 * cw].reshape(3, heads, HEAD_DIM)
    zero_p = jnp.zeros((b, cw), F32)
    rw, prompt_wkv = _wkv(lane_p(r), lane_p(k), lane_p(v), lane_p(w), lane_p(a), zero_wkv, wts,
                          b=b, t=s, h=heads, tt=min(64, s),
                          shift=(mu[0], mu[1], mu[2], zero_p, zero_p, zero_p))
    att = _attn_prompt(q, ka, va, b=b, s=s)
    y_prompt = _post(x_prompt.reshape(b * s, d), rw, g, att, p_prompt[0].reshape(b * s, -1), wts,
                     precise=False, tm=min(512, b * s), ne_step=4).reshape(b, s, d)
    prompt_k_win = ka.reshape(b, s, heads, HEAD_DIM)[:, s - keep:][None]
    prompt_v_win = va.reshape(b, s, heads, HEAD_DIM)[:, s - keep:][None]
    prompt_shift = plast[None]

    n_s = db * t
    pos_s = _rope_tables(np.tile(PAST_LEN + np.arange(t), db))
    sprev = jnp.repeat(padc(state_shift[0]), t, axis=0)
    r, k, v, w, a, g, q, ka, va, pfull = _inproj(
        x_sample.reshape(n_s, d), sprev, pos_s, wts, precise=True, seq_len=t, tm=n_s)
    lane_s = lambda z: _to_lane_layout(z, db, t, heads)
    rw, sample_wkv = _wkv(lane_s(r), lane_s(k), lane_s(v), lane_s(w), lane_s(a), state_wkv[0], wts,
                          b=db, t=t, h=heads, tt=t)
    wc = cache_k_win.shape[2]
    att = _attn_decode(q, ka, va, cache_k_win[0].reshape(db, wc, cw), cache_v_win[0].reshape(db, wc, cw),
                       b=db, t=t)
    y_sample = _post(x_sample.reshape(n_s, d), rw, g, att, p_sample[0].reshape(n_s, -1), wts,
                     precise=True, tm=n_s, ne_step=2).reshape(db, t, d)
    sample_k_rows = ka.reshape(db, t, heads, HEAD_DIM)[None]
    sample_v_rows = va.reshape(db, t, heads, HEAD_DIM)[None]
    sample_shift = _unpad_shift(pfull.reshape(db, t, cp)[:, t - 1], wts)[None]

    return (y_prompt, y_sample, prompt_k_win, prompt_v_win, prompt_wkv[None], prompt_shift,
            sample_k_rows, sample_v_rows, sample_wkv[None], sample_shift)
```
